```python
import jax
import jax.numpy as jnp
from jax import lax
import numpy as np

D_MODEL = 1024
BATCH = 2
SEQ = 8192
DEPTH = 2

N_META = 16
D_MIX = D_MODEL
CONV_WIDTH = D_MIX // 4
HG_WIDTH = D_MIX // 2
HG_HEAD_DIM = 128
HG_HEADS = HG_WIDTH // HG_HEAD_DIM
POOL_WIDTH = D_MIX - CONV_WIDTH - HG_WIDTH
POOL_WINDOWS = (2, 4, 8, 16)
POOL_GROUPS = len(POOL_WINDOWS)
POOL_GROUP_DIM = POOL_WIDTH // POOL_GROUPS
SHORT_CONV_K = 3
FFN_CONV_K = 3
D_FF = 2816
CHUNK = 64
D_IN = 3 * CONV_WIDTH + 4 * HG_WIDTH + POOL_WIDTH
ALPHA = (2 * DEPTH) ** 0.25
BETA = (8 * DEPTH) ** -0.25
LN_EPS = 1e-5
RMS_EPS = 1e-6
F_FLOOR = 1e-30
SPLIT_SIZES = (CONV_WIDTH,) * 3 + (HG_WIDTH,) * 4 + (POOL_WIDTH,)
SPLIT_IDX = tuple(int(s) for s in np.cumsum(SPLIT_SIZES)[:-1])

kernel_name = "hymba_conv_hgrn2_pool_deepnorm"


def causal_dwconv(x, w, b=None):
    K = w.shape[-1]
    L = x.shape[1]
    xp = jnp.pad(x, ((0, 0), (K - 1, 0), (0, 0)))
    y = xp[:, 0:L, :] * w[:, 0]
    for k in range(1, K):
        y = y + xp[:, k:k + L, :] * w[:, k]
    if b is not None:
        y = y + b
    return y


def layer_norm(x, g, b):
    xf = x.astype(jnp.float32)
    mu = jnp.mean(xf, axis=-1, keepdims=True)
    var = jnp.mean(jnp.square(xf - mu), axis=-1, keepdims=True)
    return ((xf - mu) * lax.rsqrt(var + LN_EPS) * g + b).astype(x.dtype)


def short_conv_mixer(bg, cg, v, w_conv):
    return bg * causal_dwconv(cg * v, w_conv)


def multiscale_pool_mixer(v, w_pool, pool_scale):
    B, L, _ = v.shape
    vf = v.astype(jnp.float32)
    c = jnp.pad(jnp.cumsum(vf, axis=1), ((0, 0), (1, 0), (0, 0)))
    t = jnp.arange(L)
    outs = []
    for gi, win in enumerate(POOL_WINDOWS):
        lo, hi = gi * POOL_GROUP_DIM, (gi + 1) * POOL_GROUP_DIM
        cg = c[..., lo:hi]
        prev = jnp.pad(cg, ((0, 0), (win, 0), (0, 0)))[:, 1:L + 1]
        count = jnp.minimum(t + 1, win).astype(jnp.float32)[:, None]
        outs.append((cg[:, 1:] - prev) / count - vf[..., lo:hi])
    d = jnp.stack(outs, axis=2)
    y = jnp.einsum('blgc,gcd->blgd', d, w_pool).reshape(B, L, POOL_WIDTH)
    return (y * pool_scale).astype(v.dtype)


def hgrn2_mixer(q, fz, i, gz, lb, g_norm):
    B, L, _ = q.shape
    f32 = jnp.float32
    fz = fz.astype(f32)
    lb = lb.astype(f32)
    sig = jax.nn.sigmoid(fz)
    f = lb + (1.0 - lb) * sig
    log_f = jnp.log(jnp.maximum(f, F_FLOOR))
    k = (1.0 - lb) * (1.0 - sig)
    pad = CHUNK - N_META

    def to_chunks(a):
        a = jnp.pad(a.astype(f32), ((0, 0), (pad, 0), (0, 0)))
        n = a.shape[1] // CHUNK
        return a.reshape(B, n, CHUNK, HG_HEADS, HG_HEAD_DIM).transpose(1, 0, 3, 2, 4)

    qc = to_chunks(q.astype(f32) * (HG_HEAD_DIM ** -0.5))
    kc, ic, gc = to_chunks(k), to_chunks(i), to_chunks(log_f)
    mask = jnp.tril(jnp.ones((CHUNK, CHUNK), dtype=bool))[:, :, None]

    def step(S, xs):
        qb, kb, ib, gb = xs
        G = jnp.cumsum(gb, axis=2)
        o_inter = jnp.einsum('bhtd,bhde->bhte', qb * jnp.exp(G), S)
        diff = G[:, :, :, None, :] - G[:, :, None, :, :]
        decay = jnp.where(mask, jnp.exp(jnp.where(mask, diff, 0.0)), 0.0)
        A = jnp.einsum('bhtd,bhsd,bhtsd->bhts', qb, kb, decay)
        o_intra = jnp.einsum('bhts,bhse->bhte', A, ib)
        G_last = G[:, :, -1:, :]
        S_new = jnp.exp(G_last[:, :, 0, :, None]) * S + jnp.einsum(
            'bhsd,bhse->bhde', kb * jnp.exp(G_last - G), ib)
        return S_new, o_inter + o_intra

    S0 = jnp.zeros((B, HG_HEADS, HG_HEAD_DIM, HG_HEAD_DIM), f32)
    _, o = lax.scan(step, S0, (qc, kc, ic, gc))
    o = o.transpose(1, 0, 3, 2, 4).reshape(B, -1, HG_HEADS, HG_HEAD_DIM)[:, pad:]
    o = o * lax.rsqrt(jnp.mean(jnp.square(o), axis=-1, keepdims=True) + RMS_EPS) * g_norm
    o = o * jax.nn.silu(gz.astype(f32).reshape(B, L, HG_HEADS, HG_HEAD_DIM))
    return o.reshape(B, L, HG_WIDTH).astype(q.dtype)


def hybrid_layer(x, lb, w_in, w_conv, w_pool, pool_scale, hg_norm_g, w_o, ln1_g, ln1_b,
                 w_up, w_ffn_conv, b_ffn_conv, w_down, ln2_g, ln2_b):
    h = x @ w_in
    cb, cc, cv, hq, hf, hi, hgate, pv = jnp.split(h, SPLIT_IDX, axis=-1)
    y_conv = short_conv_mixer(cb, cc, cv, w_conv)
    y_hg = hgrn2_mixer(hq, hf, hi, hgate, lb, hg_norm_g)
    y_pool = multiscale_pool_mixer(pv, w_pool, pool_scale)
    mix = jnp.concatenate([y_conv, y_hg, y_pool], axis=-1) @ w_o
    x = layer_norm(ALPHA * x + mix, ln1_g, ln1_b)
    u = causal_dwconv(x @ w_up, w_ffn_conv, b_ffn_conv)
    gate, val = jnp.split(u, 2, axis=-1)
    ffn = (jax.nn.silu(gate) * val) @ w_down
    return layer_norm(ALPHA * x + ffn, ln2_g, ln2_b)


def setup_inputs(seed: int = 0) -> dict:
    key = jax.random.key(seed)
    ks = jax.random.split(key, 17)
    f32 = jnp.float32

    def nrm(k, shape):
        return jax.random.normal(k, shape, f32)

    col_scale = jnp.concatenate([
        jnp.ones((2 * CONV_WIDTH,), f32), jnp.full((CONV_WIDTH,), BETA, f32),
        jnp.ones((2 * HG_WIDTH,), f32), jnp.full((HG_WIDTH,), BETA, f32),
        jnp.ones((HG_WIDTH,), f32), jnp.full((POOL_WIDTH,), BETA, f32)])
    return {
        'x': nrm(ks[0], (BATCH, SEQ, D_MODEL)),
        'meta_tokens': nrm(ks[1], (N_META, D_MODEL)),
        'hg_lower_bounds': 0.1 * nrm(ks[2], (DEPTH, HG_WIDTH)),
        'w_in': nrm(ks[3], (DEPTH, D_MODEL, D_IN)) * (D_MODEL ** -0.5) * col_scale,
        'w_conv': nrm(ks[4], (DEPTH, CONV_WIDTH, SHORT_CONV_K)) * (SHORT_CONV_K ** -0.5),
        'w_pool': nrm(ks[5], (DEPTH, POOL_GROUPS, POOL_GROUP_DIM, POOL_GROUP_DIM)) * (POOL_GROUP_DIM ** -0.5),
        'pool_scale': 1.0 + 0.02 * nrm(ks[6], (DEPTH, POOL_WIDTH)),
        'hg_norm_g': 1.0 + 0.02 * nrm(ks[7], (DEPTH, HG_HEAD_DIM)),
        'w_o': nrm(ks[8], (DEPTH, D_MIX, D_MODEL)) * (D_MIX ** -0.5) * BETA,
        'ln1_g': 1.0 + 0.02 * nrm(ks[9], (DEPTH, D_MODEL)),
        'ln1_b': 0.02 * nrm(ks[10], (DEPTH, D_MODEL)),
        'w_up': nrm(ks[11], (DEPTH, D_MODEL, 2 * D_FF)) * (D_MODEL ** -0.5),
        'w_ffn_conv': nrm(ks[12], (DEPTH, 2 * D_FF, FFN_CONV_K)) * (FFN_CONV_K ** -0.5),
        'b_ffn_conv': 0.02 * nrm(ks[13], (DEPTH, 2 * D_FF)),
        'w_down': nrm(ks[14], (DEPTH, D_FF, D_MODEL)) * (D_FF ** -0.5) * BETA,
        'ln2_g': 1.0 + 0.02 * nrm(ks[15], (DEPTH, D_MODEL)),
        'ln2_b': 0.02 * nrm(ks[16], (DEPTH, D_MODEL)),
    }


def reference(x, meta_tokens, hg_lower_bounds, w_in, w_conv, w_pool, pool_scale, hg_norm_g,
              w_o, ln1_g, ln1_b, w_up, w_ffn_conv, b_ffn_conv, w_down, ln2_g, ln2_b):
    B = x.shape[0]
    meta = jnp.broadcast_to(meta_tokens[None].astype(x.dtype), (B, N_META, D_MODEL))
    h = jnp.concatenate([meta, x], axis=1)
    p = jax.nn.softmax(hg_lower_bounds.astype(jnp.float32), axis=0)
    lbs = jnp.cumsum(p, axis=0) - p[0]
    for l in range(DEPTH):
        h = hybrid_layer(h, lbs[l], w_in[l], w_conv[l], w_pool[l], pool_scale[l], hg_norm_g[l],
                         w_o[l], ln1_g[l], ln1_b[l], w_up[l], w_ffn_conv[l], b_ffn_conv[l],
                         w_down[l], ln2_g[l], ln2_b[l])
    return h[:, N_META:]
```

```python
import functools

import jax
import jax.numpy as jnp
from jax import lax
from jax.experimental import pallas as pl
from jax.experimental.pallas import tpu as pltpu

D_MODEL = 1024
N_META = 16
CONV_WIDTH = 256
HG_WIDTH = 512
HG_HEAD_DIM = 128
HG_HEADS = 4
POOL_WIDTH = 256
POOL_WINDOWS = (2, 4, 8, 16)
POOL_GROUP_DIM = 64
D_FF = 2816
D_IN = 3 * CONV_WIDTH + 4 * HG_WIDTH + POOL_WIDTH
DEPTH = 2
ALPHA = (2 * DEPTH) ** 0.25
LN_EPS = 1e-5
RMS_EPS = 1e-6
F_FLOOR = 1e-30
NEG_BIG = -1e30

ROW_TILE = 432
CHUNK = 16
HALF = 8
FF_CHUNK = 256
N_FF_CHUNKS = D_FF // FF_CHUNK
VMEM_LIMIT_BYTES = 56 * 1024 * 1024

_C_CONV = 0
_C_HG = 3 * CONV_WIDTH
_C_POOL = _C_HG + 4 * HG_WIDTH


def _layer_norm(r, g, b):
    mu = jnp.mean(r, axis=-1, keepdims=True)
    c = r - mu
    var = jnp.mean(c * c, axis=-1, keepdims=True)
    return c * lax.rsqrt(var + LN_EPS) * g + b


def _sigmoid(x):
    return 1.0 / (1.0 + jnp.exp(-x))


def _mixer_kernel(layer, x_ref, hb_ref, win_ref, wconv_ref, wpool_ref, pscale_ref, gnorm_ref,
                  wo_ref, lng_ref, lnb_ref, out_ref,
                  zbuf, vbuf, b2, b4, b8, st_ref, q_s, k_s, i_s, g_s, o_s, ymix):
    f32, bf16 = jnp.float32, jnp.bfloat16
    tm = ROW_TILE
    step = pl.program_id(1)

    @pl.when(step == 0)
    def _():
        zbuf[0:8, :] = jnp.zeros((8, CONV_WIDTH), f32)
        vbuf[0:16, :] = jnp.zeros((16, POOL_WIDTH), f32)
        st_ref[...] = jnp.zeros(st_ref.shape, f32)

    @pl.when(step > 0)
    def _():
        zbuf[0:8, :] = zbuf[tm:tm + 8, :]
        vbuf[0:16, :] = vbuf[tm:tm + 16, :]

    x = x_ref[0]
    xb = x.astype(bf16)

    hc = jnp.dot(xb, win_ref[:, _C_CONV:_C_HG], preferred_element_type=f32)
    cb = hc[:, 0:CONV_WIDTH]
    zbuf[8:8 + tm, :] = hc[:, CONV_WIDTH:2 * CONV_WIDTH] * hc[:, 2 * CONV_WIDTH:3 * CONV_WIDTH]
    wc = wconv_ref[...]
    conv = (wc[0:1] * zbuf[6:6 + tm, :] + wc[1:2] * zbuf[7:7 + tm, :] + wc[2:3] * zbuf[8:8 + tm, :])
    ymix[:, 0:CONV_WIDTH] = (cb * conv).astype(bf16)

    pv = jnp.dot(xb, win_ref[:, _C_POOL:D_IN], preferred_element_type=f32)
    vbuf[16:16 + tm, :] = pv
    b2[0:tm + 14, :] = vbuf[2:16 + tm, :] + vbuf[1:15 + tm, :]
    b4[0:tm + 12, :] = b2[2:tm + 14, :] + b2[0:tm + 12, :]
    b8[0:tm + 8, :] = b4[4:tm + 12, :] + b4[0:tm + 8, :]
    s2 = b2[14:14 + tm, :]
    s4 = b4[12:12 + tm, :]
    s8 = b8[8:8 + tm, :]
    s16 = s8 + b8[0:tm, :]
    lane = lax.broadcasted_iota(jnp.int32, (1, POOL_WIDTH), 1)
    g0, g1, g2 = lane < POOL_GROUP_DIM, lane < 2 * POOL_GROUP_DIM, lane < 3 * POOL_GROUP_DIM
    win = jnp.where(g0, 2.0, jnp.where(g1, 4.0, jnp.where(g2, 8.0, 16.0))).astype(f32)
    wsum = jnp.where(g0, s2, jnp.where(g1, s4, jnp.where(g2, s8, s16)))
    pos = (step * tm + lax.broadcasted_iota(jnp.int32, (tm, 1), 0)).astype(f32)
    count = jnp.minimum(pos + 1.0, win)
    dpool = wsum / count - pv
    ypool = jnp.dot(dpool.astype(bf16), wpool_ref[...], preferred_element_type=f32) * pscale_ref[...]
    ymix[:, CONV_WIDTH + HG_WIDTH:D_MODEL] = ypool.astype(bf16)

    hh = jnp.dot(xb, win_ref[:, _C_HG:_C_POOL], preferred_element_type=f32)
    hq = hh[:, 0:HG_WIDTH]
    fz = hh[:, HG_WIDTH:2 * HG_WIDTH]
    hi = hh[:, 2 * HG_WIDTH:3 * HG_WIDTH]
    gz = hh[:, 3 * HG_WIDTH:4 * HG_WIDTH]

    hb = hb_ref[...]
    e = jnp.exp(hb - jnp.max(hb, axis=0, keepdims=True))
    p = e / jnp.sum(e, axis=0, keepdims=True)
    lb = jnp.sum(p[0:layer + 1], axis=0, keepdims=True) - p[0:1]

    sig = _sigmoid(fz)
    f = lb + (1.0 - lb) * sig
    g = jnp.log(jnp.maximum(f, F_FLOOR))
    rowmod = lax.broadcasted_iota(jnp.int32, (tm, HG_WIDTH), 0) & (CHUNK - 1)
    sh = 1
    while sh < CHUNK:
        g = g + jnp.where(rowmod >= sh, pltpu.roll(g, sh, 0), 0.0)
        sh *= 2
    q_s[...] = hq * (HG_HEAD_DIM ** -0.5)
    k_s[...] = (1.0 - lb) * (1.0 - sig)
    i_s[...] = hi
    g_s[...] = g

    rid = lax.broadcasted_iota(jnp.int32, (HALF, 1), 0)

    def chunk_body(c, carry):
        r0 = pl.multiple_of(c * CHUNK, CHUNK)
        rows = pl.ds(r0, CHUNK)
        for h in range(HG_HEADS):
            ls = slice(h * HG_HEAD_DIM, (h + 1) * HG_HEAD_DIM)
            gc = g_s[rows, ls]
            qc = q_s[rows, ls]
            kc = k_s[rows, ls]
            ic = i_s[rows, ls]
            gl = gc[CHUNK - 1:CHUNK]
            st = st_ref[h]
            qt = (qc * jnp.exp(gc)).astype(bf16)
            o_inter = lax.dot_general(qt, st.astype(bf16), (((1,), (1,)), ((), ())),
                                      preferred_element_type=f32)
            q_top, q_bot = qc[0:HALF], qc[HALF:CHUNK]
            g_top, g_bot = gc[0:HALF], gc[HALF:CHUNK]
            o_top = jnp.zeros((HALF, HG_HEAD_DIM), f32)
            o_bot = jnp.zeros((HALF, HG_HEAD_DIM), f32)
            for s in range(CHUNK):
                ks = kc[s:s + 1]
                gs = gc[s:s + 1]
                is_ = ic[s:s + 1]
                if s < HALF:
                    d_top = jnp.where(rid >= s, g_top - gs, NEG_BIG)
                    a_top = jnp.sum(q_top * ks * jnp.exp(d_top), axis=-1, keepdims=True)
                    o_top = o_top + a_top * is_
                    d_bot = g_bot - gs
                else:
                    d_bot = jnp.where(rid >= s - HALF, g_bot - gs, NEG_BIG)
                a_bot = jnp.sum(q_bot * ks * jnp.exp(d_bot), axis=-1, keepdims=True)
                o_bot = o_bot + a_bot * is_
            o_s[pl.ds(r0, HALF), ls] = o_inter[0:HALF] + o_top
            o_s[pl.ds(r0 + HALF, HALF), ls] = o_inter[HALF:CHUNK] + o_bot
            kt = (kc * jnp.exp(gl - gc)).astype(bf16)
            delta = lax.dot_general(ic.astype(bf16), kt, (((0,), (0,)), ((), ())),
                                    preferred_element_type=f32)
            st_ref[h] = jnp.exp(gl) * st + delta
        return carry

    lax.fori_loop(0, tm // CHUNK, chunk_body, 0)

    gn = gnorm_ref[...]
    for h in range(HG_HEADS):
        ls = slice(h * HG_HEAD_DIM, (h + 1) * HG_HEAD_DIM)
        o = o_s[:, ls]
        o = o * lax.rsqrt(jnp.mean(o * o, axis=-1, keepdims=True) + RMS_EPS) * gn
        gate = gz[:, ls]
        o = o * (gate * _sigmoid(gate))
        ymix[:, CONV_WIDTH + h * HG_HEAD_DIM:CONV_WIDTH + (h + 1) * HG_HEAD_DIM] = o.astype(bf16)

    mix = jnp.dot(ymix[...], wo_ref[...], preferred_element_type=f32)
    out_ref[0] = _layer_norm(ALPHA * x + mix, lng_ref[...], lnb_ref[...])


def _ffn_kernel(x_ref, wup_ref, wcv_ref, bcv_ref, wdn_ref, lng_ref, lnb_ref, out_ref,
                xb_s, ubuf, ucarry, acc):
    f32, bf16 = jnp.float32, jnp.bfloat16
    tm = ROW_TILE
    step = pl.program_id(1)

    @pl.when(step == 0)
    def _():
        ucarry[...] = jnp.zeros(ucarry.shape, f32)

    xb_s[...] = x_ref[0].astype(bf16)
    acc[...] = jnp.zeros(acc.shape, f32)

    def body(j, carry):
        u = jnp.dot(xb_s[...], wup_ref[j], preferred_element_type=f32)
        ubuf[0:8, :] = ucarry[j]
        ubuf[8:8 + tm, :] = u
        ucarry[j] = u[tm - 8:tm]
        w = wcv_ref[j]
        y = w[0:1] * ubuf[6:6 + tm, :] + w[1:2] * ubuf[7:7 + tm, :] + w[2:3] * u + bcv_ref[j]
        gate = y[:, 0:FF_CHUNK]
        val = y[:, FF_CHUNK:2 * FF_CHUNK]
        a = (gate * _sigmoid(gate) * val).astype(bf16)
        acc[...] += jnp.dot(a, wdn_ref[j], preferred_element_type=f32)
        return carry

    lax.fori_loop(0, N_FF_CHUNKS, body, 0)
    out_ref[0] = _layer_norm(ALPHA * x_ref[0] + acc[...], lng_ref[...], lnb_ref[...])


def _const_spec(shape):
    zeros = (0,) * len(shape)
    return pl.BlockSpec(shape, lambda b, i: zeros, pipeline_mode=pl.Buffered(1))


def _row_spec():
    return pl.BlockSpec((1, ROW_TILE, D_MODEL), lambda b, i: (b, i, 0))


def _compiler_params():
    return pltpu.CompilerParams(dimension_semantics=("arbitrary", "arbitrary"),
                                vmem_limit_bytes=VMEM_LIMIT_BYTES)


def _mixer_call(layer, h, hb, win, wconv, wpool, pscale, gnorm, wo, lng, lnb):
    batch, rows, _ = h.shape
    tm = ROW_TILE
    f32 = jnp.float32
    return pl.pallas_call(
        functools.partial(_mixer_kernel, layer),
        grid=(batch, rows // tm),
        in_specs=[_row_spec(), _const_spec(hb.shape), _const_spec(win.shape), _const_spec(wconv.shape),
                  _const_spec(wpool.shape), _const_spec(pscale.shape), _const_spec(gnorm.shape),
                  _const_spec(wo.shape), _const_spec(lng.shape), _const_spec(lnb.shape)],
        out_specs=_row_spec(),
        out_shape=jax.ShapeDtypeStruct(h.shape, f32),
        scratch_shapes=[
            pltpu.VMEM((tm + 8, CONV_WIDTH), f32),
            pltpu.VMEM((tm + 16, POOL_WIDTH), f32),
            pltpu.VMEM((tm + 16, POOL_WIDTH), f32),
            pltpu.VMEM((tm + 16, POOL_WIDTH), f32),
            pltpu.VMEM((tm + 16, POOL_WIDTH), f32),
            pltpu.VMEM((HG_HEADS, HG_HEAD_DIM, HG_HEAD_DIM), f32),
            pltpu.VMEM((tm, HG_WIDTH), f32),
            pltpu.VMEM((tm, HG_WIDTH), f32),
            pltpu.VMEM((tm, HG_WIDTH), f32),
            pltpu.VMEM((tm, HG_WIDTH), f32),
            pltpu.VMEM((tm, HG_WIDTH), f32),
            pltpu.VMEM((tm, D_MODEL), jnp.bfloat16),
        ],
        compiler_params=_compiler_params(),
        name=f"mixer_l{layer}",
    )(h, hb, win, wconv, wpool, pscale, gnorm, wo, lng, lnb)


def _ffn_call(layer, h, wup, wcv, bcv, wdn, lng, lnb):
    batch, rows, _ = h.shape
    tm = ROW_TILE
    f32 = jnp.float32
    return pl.pallas_call(
        _ffn_kernel,
        grid=(batch, rows // tm),
        in_specs=[_row_spec(), _const_spec(wup.shape), _const_spec(wcv.shape), _const_spec(bcv.shape),
                  _const_spec(wdn.shape), _const_spec(lng.shape), _const_spec(lnb.shape)],
        out_specs=_row_spec(),
        out_shape=jax.ShapeDtypeStruct(h.shape, f32),
        scratch_shapes=[
            pltpu.VMEM((tm, D_MODEL), jnp.bfloat16),
            pltpu.VMEM((tm + 8, 2 * FF_CHUNK), f32),
            pltpu.VMEM((N_FF_CHUNKS, 8, 2 * FF_CHUNK), f32),
            pltpu.VMEM((tm, D_MODEL), f32),
        ],
        compiler_params=_compiler_params(),
        name=f"ffn_l{layer}",
    )(h, wup, wcv, bcv, wdn, lng, lnb)


def _chunk_cols(a):
    lead = a.shape[:-1]
    a = a.reshape(lead + (2, N_FF_CHUNKS, FF_CHUNK))
    a = jnp.moveaxis(a, -2, 0)
    return a.reshape((N_FF_CHUNKS,) + lead + (2 * FF_CHUNK,))


def kernel(x, meta_tokens, hg_lower_bounds, w_in, w_conv, w_pool, pool_scale, hg_norm_g, w_o, ln1_g, ln1_b,
           w_up, w_ffn_conv, b_ffn_conv, w_down, ln2_g, ln2_b):
    bf16 = jnp.bfloat16
    batch = x.shape[0]
    assert (N_META + x.shape[1]) % ROW_TILE == 0 and ROW_TILE % CHUNK == 0
    meta = jnp.broadcast_to(meta_tokens[None].astype(x.dtype), (batch, N_META, D_MODEL))
    h = jnp.concatenate([meta, x], axis=1)
    hb = hg_lower_bounds.astype(jnp.float32)
    for l in range(DEPTH):
        wpool_bd = jax.scipy.linalg.block_diag(*[w_pool[l, g] for g in range(len(POOL_WINDOWS))])
        h = _mixer_call(l, h, hb, w_in[l].astype(bf16), w_conv[l].T, wpool_bd.astype(bf16),
                        pool_scale[l][None], hg_norm_g[l][None], w_o[l].astype(bf16),
                        ln1_g[l][None], ln1_b[l][None])
        h = _ffn_call(l, h, _chunk_cols(w_up[l]).astype(bf16), _chunk_cols(w_ffn_conv[l].T),
                      _chunk_cols(b_ffn_conv[l][None]), w_down[l].reshape(N_FF_CHUNKS, FF_CHUNK, D_MODEL).astype(bf16),
                      ln2_g[l][None], ln2_b[l][None])
    return h[:, N_META:]
```

```python
import functools

import jax
import jax.numpy as jnp
from jax import lax
from jax.experimental import pallas as pl
from jax.experimental.pallas import tpu as pltpu

D_MODEL = 1024
N_META = 16
CONV_WIDTH = 256
HG_WIDTH = 512
HG_HEAD_DIM = 128
HG_HEADS = 4
POOL_WIDTH = 256
POOL_WINDOWS = (2, 4, 8, 16)
POOL_GROUP_DIM = 64
D_FF = 2816
D_IN = 3 * CONV_WIDTH + 4 * HG_WIDTH + POOL_WIDTH
DEPTH = 2
ALPHA = (2 * DEPTH) ** 0.25
LN_EPS = 1e-5
RMS_EPS = 1e-6
F_FLOOR = 1e-30
NEG_BIG = -1e30

ROW_TILE = 432
CHUNK = 16
HALF = 8
FF_CHUNK = 256
FF_GROUP = 4
N_FF_CHUNKS = D_FF // FF_CHUNK
VMEM_LIMIT_BYTES = 56 * 1024 * 1024

_C_CONV = 0
_C_HG = 3 * CONV_WIDTH
_C_POOL = _C_HG + 4 * HG_WIDTH


def _layer_norm(r, g, b):
    mu = jnp.mean(r, axis=-1, keepdims=True)
    c = r - mu
    var = jnp.mean(c * c, axis=-1, keepdims=True)
    return c * lax.rsqrt(var + LN_EPS) * g + b


def _sigmoid(x):
    return 1.0 / (1.0 + jnp.exp(-x))


def _mixer_kernel(layer, x_ref, hb_ref, win_ref, wconv_ref, wpool_ref, pscale_ref, gnorm_ref,
                  wo_ref, lng_ref, lnb_ref, out_ref,
                  zbuf, vbuf, b2, b4, b8, st_ref, q_s, k_s, i_s, g_s, o_s, ymix):
    f32, bf16 = jnp.float32, jnp.bfloat16
    tm = ROW_TILE
    step = pl.program_id(1)

    @pl.when(step == 0)
    def _():
        zbuf[0:8, :] = jnp.zeros((8, CONV_WIDTH), f32)
        vbuf[0:16, :] = jnp.zeros((16, POOL_WIDTH), f32)
        st_ref[...] = jnp.zeros(st_ref.shape, f32)

    @pl.when(step > 0)
    def _():
        zbuf[0:8, :] = zbuf[tm:tm + 8, :]
        vbuf[0:16, :] = vbuf[tm:tm + 16, :]

    x = x_ref[0]
    xb = x.astype(bf16)

    hc = jnp.dot(xb, win_ref[:, _C_CONV:_C_HG], preferred_element_type=f32)
    cb = hc[:, 0:CONV_WIDTH]
    zbuf[8:8 + tm, :] = hc[:, CONV_WIDTH:2 * CONV_WIDTH] * hc[:, 2 * CONV_WIDTH:3 * CONV_WIDTH]
    wc = wconv_ref[...]
    conv = (wc[0:1] * zbuf[6:6 + tm, :] + wc[1:2] * zbuf[7:7 + tm, :] + wc[2:3] * zbuf[8:8 + tm, :])
    ymix[:, 0:CONV_WIDTH] = (cb * conv).astype(bf16)

    pv = jnp.dot(xb, win_ref[:, _C_POOL:D_IN], preferred_element_type=f32)
    vbuf[16:16 + tm, :] = pv
    b2[0:tm + 14, :] = vbuf[2:16 + tm, :] + vbuf[1:15 + tm, :]
    b4[0:tm + 12, :] = b2[2:tm + 14, :] + b2[0:tm + 12, :]
    b8[0:tm + 8, :] = b4[4:tm + 12, :] + b4[0:tm + 8, :]
    s2 = b2[14:14 + tm, :]
    s4 = b4[12:12 + tm, :]
    s8 = b8[8:8 + tm, :]
    s16 = s8 + b8[0:tm, :]
    lane = lax.broadcasted_iota(jnp.int32, (1, POOL_WIDTH), 1)
    g0, g1, g2 = lane < POOL_GROUP_DIM, lane < 2 * POOL_GROUP_DIM, lane < 3 * POOL_GROUP_DIM
    win = jnp.where(g0, 2.0, jnp.where(g1, 4.0, jnp.where(g2, 8.0, 16.0))).astype(f32)
    wsum = jnp.where(g0, s2, jnp.where(g1, s4, jnp.where(g2, s8, s16)))
    pos = (step * tm + lax.broadcasted_iota(jnp.int32, (tm, 1), 0)).astype(f32)
    count = jnp.minimum(pos + 1.0, win)
    dpool = wsum / count - pv
    ypool = jnp.dot(dpool.astype(bf16), wpool_ref[...], preferred_element_type=f32) * pscale_ref[...]
    ymix[:, CONV_WIDTH + HG_WIDTH:D_MODEL] = ypool.astype(bf16)

    hh = jnp.dot(xb, win_ref[:, _C_HG:_C_POOL], preferred_element_type=f32)
    hq = hh[:, 0:HG_WIDTH]
    fz = hh[:, HG_WIDTH:2 * HG_WIDTH]
    hi = hh[:, 2 * HG_WIDTH:3 * HG_WIDTH]
    gz = hh[:, 3 * HG_WIDTH:4 * HG_WIDTH]

    hb = hb_ref[...]
    e = jnp.exp(hb - jnp.max(hb, axis=0, keepdims=True))
    p = e / jnp.sum(e, axis=0, keepdims=True)
    lb = jnp.sum(p[0:layer + 1], axis=0, keepdims=True) - p[0:1]

    sig = _sigmoid(fz)
    f = lb + (1.0 - lb) * sig
    g = jnp.log(jnp.maximum(f, F_FLOOR))
    rowmod = lax.broadcasted_iota(jnp.int32, (tm, HG_WIDTH), 0) & (CHUNK - 1)
    sh = 1
    while sh < CHUNK:
        g = g + jnp.where(rowmod >= sh, pltpu.roll(g, sh, 0), 0.0)
        sh *= 2
    q_s[...] = hq * (HG_HEAD_DIM ** -0.5)
    k_s[...] = (1.0 - lb) * (1.0 - sig)
    i_s[...] = hi
    g_s[...] = g

    rid = lax.broadcasted_iota(jnp.int32, (HALF, 1), 0)

    def chunk_body(c, carry):
        r0 = pl.multiple_of(c * CHUNK, CHUNK)
        rows = pl.ds(r0, CHUNK)
        for h in range(HG_HEADS):
            ls = slice(h * HG_HEAD_DIM, (h + 1) * HG_HEAD_DIM)
            gc = g_s[rows, ls]
            qc = q_s[rows, ls]
            kc = k_s[rows, ls]
            ic = i_s[rows, ls]
            gl = gc[CHUNK - 1:CHUNK]
            st = st_ref[h]
            qt = (qc * jnp.exp(gc)).astype(bf16)
            o_inter = lax.dot_general(qt, st.astype(bf16), (((1,), (1,)), ((), ())),
                                      preferred_element_type=f32)
            q_top, q_bot = qc[0:HALF], qc[HALF:CHUNK]
            g_top, g_bot = gc[0:HALF], gc[HALF:CHUNK]
            o_top = jnp.zeros((HALF, HG_HEAD_DIM), f32)
            o_bot = jnp.zeros((HALF, HG_HEAD_DIM), f32)
            for s in range(CHUNK):
                ks = kc[s:s + 1]
                gs = gc[s:s + 1]
                is_ = ic[s:s + 1]
                if s < HALF:
                    d_top = jnp.where(rid >= s, g_top - gs, NEG_BIG)
                    a_top = jnp.sum(q_top * ks * jnp.exp(d_top), axis=-1, keepdims=True)
                    o_top = o_top + a_top * is_
                    d_bot = g_bot - gs
                else:
                    d_bot = jnp.where(rid >= s - HALF, g_bot - gs, NEG_BIG)
                a_bot = jnp.sum(q_bot * ks * jnp.exp(d_bot), axis=-1, keepdims=True)
                o_bot = o_bot + a_bot * is_
            o_s[pl.ds(r0, HALF), ls] = o_inter[0:HALF] + o_top
            o_s[pl.ds(r0 + HALF, HALF), ls] = o_inter[HALF:CHUNK] + o_bot
            kt = (kc * jnp.exp(gl - gc)).astype(bf16)
            delta = lax.dot_general(ic.astype(bf16), kt, (((0,), (0,)), ((), ())),
                                    preferred_element_type=f32)
            st_ref[h] = jnp.exp(gl) * st + delta
        return carry

    lax.fori_loop(0, tm // CHUNK, chunk_body, 0)

    gn = gnorm_ref[...]
    for h in range(HG_HEADS):
        ls = slice(h * HG_HEAD_DIM, (h + 1) * HG_HEAD_DIM)
        o = o_s[:, ls]
        o = o * lax.rsqrt(jnp.mean(o * o, axis=-1, keepdims=True) + RMS_EPS) * gn
        gate = gz[:, ls]
        o = o * (gate * _sigmoid(gate))
        ymix[:, CONV_WIDTH + h * HG_HEAD_DIM:CONV_WIDTH + (h + 1) * HG_HEAD_DIM] = o.astype(bf16)

    mix = jnp.dot(ymix[...], wo_ref[...], preferred_element_type=f32)
    out_ref[0] = _layer_norm(ALPHA * x + mix, lng_ref[...], lnb_ref[...])


def _ffn_kernel(x_ref, wup_ref, wcv_ref, bcv_ref, wdn_ref, lng_ref, lnb_ref, out_ref,
                xb_s, ubuf, ucarry, a_s, acc):
    f32, bf16 = jnp.float32, jnp.bfloat16
    tm = ROW_TILE
    step = pl.program_id(1)

    @pl.when(step == 0)
    def _():
        ucarry[...] = jnp.zeros(ucarry.shape, f32)

    xb_s[...] = x_ref[0].astype(bf16)

    def up_project(j):
        ubuf[j % 2] = jnp.dot(xb_s[...], wup_ref[j], preferred_element_type=f32)

    def shifted(u, prev8, k):
        hist = jnp.concatenate([prev8, u[0:8]], axis=0)
        head = pltpu.roll(hist, k, 0)[8:16]
        return jnp.concatenate([head, pltpu.roll(u, k, 0)[8:]], axis=0)

    up_project(0)
    partial = None
    for j in range(N_FF_CHUNKS):
        if j + 1 < N_FF_CHUNKS:
            up_project(j + 1)
        u = ubuf[j % 2]
        prev8 = ucarry[j]
        ucarry[j] = u[tm - 8:tm]
        w = wcv_ref[j]
        y = w[0:1] * shifted(u, prev8, 2) + w[1:2] * shifted(u, prev8, 1) + w[2:3] * u + bcv_ref[j]
        gate = y[:, 0:FF_CHUNK]
        val = y[:, FF_CHUNK:2 * FF_CHUNK]
        grp, k = divmod(j, FF_GROUP)
        a_g = a_s.at[grp % 2]
        a_g[:, k * FF_CHUNK:(k + 1) * FF_CHUNK] = (gate * _sigmoid(gate) * val).astype(bf16)
        if k == FF_GROUP - 1 or j == N_FF_CHUNKS - 1:
            j0 = j - k
            partial = jnp.dot(a_g[:, 0:(k + 1) * FF_CHUNK], wdn_ref[j0 * FF_CHUNK:(j + 1) * FF_CHUNK, :],
                              preferred_element_type=f32)
            if j == N_FF_CHUNKS - 1:
                break
            if j0 == 0:
                acc[...] = partial
            else:
                acc[...] += partial
    out_ref[0] = _layer_norm(ALPHA * x_ref[0] + acc[...] + partial, lng_ref[...], lnb_ref[...])


def _const_spec(shape):
    zeros = (0,) * len(shape)
    return pl.BlockSpec(shape, lambda b, i: zeros, pipeline_mode=pl.Buffered(1))


def _row_spec():
    return pl.BlockSpec((1, ROW_TILE, D_MODEL), lambda b, i: (b, i, 0))


def _compiler_params():
    return pltpu.CompilerParams(dimension_semantics=("arbitrary", "arbitrary"),
                                vmem_limit_bytes=VMEM_LIMIT_BYTES)


def _mixer_call(layer, h, hb, win, wconv, wpool, pscale, gnorm, wo, lng, lnb):
    batch, rows, _ = h.shape
    tm = ROW_TILE
    f32 = jnp.float32
    return pl.pallas_call(
        functools.partial(_mixer_kernel, layer),
        grid=(batch, rows // tm),
        in_specs=[_row_spec(), _const_spec(hb.shape), _const_spec(win.shape), _const_spec(wconv.shape),
                  _const_spec(wpool.shape), _const_spec(pscale.shape), _const_spec(gnorm.shape),
                  _const_spec(wo.shape), _const_spec(lng.shape), _const_spec(lnb.shape)],
        out_specs=_row_spec(),
        out_shape=jax.ShapeDtypeStruct(h.shape, f32),
        scratch_shapes=[
            pltpu.VMEM((tm + 8, CONV_WIDTH), f32),
            pltpu.VMEM((tm + 16, POOL_WIDTH), f32),
            pltpu.VMEM((tm + 16, POOL_WIDTH), f32),
            pltpu.VMEM((tm + 16, POOL_WIDTH), f32),
            pltpu.VMEM((tm + 16, POOL_WIDTH), f32),
            pltpu.VMEM((HG_HEADS, HG_HEAD_DIM, HG_HEAD_DIM), f32),
            pltpu.VMEM((tm, HG_WIDTH), f32),
            pltpu.VMEM((tm, HG_WIDTH), f32),
            pltpu.VMEM((tm, HG_WIDTH), f32),
            pltpu.VMEM((tm, HG_WIDTH), f32),
            pltpu.VMEM((tm, HG_WIDTH), f32),
            pltpu.VMEM((tm, D_MODEL), jnp.bfloat16),
        ],
        compiler_params=_compiler_params(),
        name=f"mixer_l{layer}",
    )(h, hb, win, wconv, wpool, pscale, gnorm, wo, lng, lnb)


def _ffn_call(layer, h, wup, wcv, bcv, wdn, lng, lnb):
    batch, rows, _ = h.shape
    tm = ROW_TILE
    f32 = jnp.float32
    return pl.pallas_call(
        _ffn_kernel,
        grid=(batch, rows // tm),
        in_specs=[_row_spec(), _const_spec(wup.shape), _const_spec(wcv.shape), _const_spec(bcv.shape),
                  _const_spec(wdn.shape), _const_spec(lng.shape), _const_spec(lnb.shape)],
        out_specs=_row_spec(),
        out_shape=jax.ShapeDtypeStruct(h.shape, f32),
        scratch_shapes=[
            pltpu.VMEM((tm, D_MODEL), jnp.bfloat16),
            pltpu.VMEM((2, tm, 2 * FF_CHUNK), f32),
            pltpu.VMEM((N_FF_CHUNKS, 8, 2 * FF_CHUNK), f32),
            pltpu.VMEM((2, tm, FF_GROUP * FF_CHUNK), jnp.bfloat16),
            pltpu.VMEM((tm, D_MODEL), f32),
        ],
        compiler_params=_compiler_params(),
        name=f"ffn_l{layer}",
    )(h, wup, wcv, bcv, wdn, lng, lnb)


def _chunk_cols(a):
    lead = a.shape[:-1]
    a = a.reshape(lead + (2, N_FF_CHUNKS, FF_CHUNK))
    a = jnp.moveaxis(a, -2, 0)
    return a.reshape((N_FF_CHUNKS,) + lead + (2 * FF_CHUNK,))


def kernel(x, meta_tokens, hg_lower_bounds, w_in, w_conv, w_pool, pool_scale, hg_norm_g, w_o, ln1_g, ln1_b,
           w_up, w_ffn_conv, b_ffn_conv, w_down, ln2_g, ln2_b):
    bf16 = jnp.bfloat16
    batch = x.shape[0]
    assert (N_META + x.shape[1]) % ROW_TILE == 0 and ROW_TILE % CHUNK == 0
    meta = jnp.broadcast_to(meta_tokens[None].astype(x.dtype), (batch, N_META, D_MODEL))
    h = jnp.concatenate([meta, x], axis=1)
    hb = hg_lower_bounds.astype(jnp.float32)
    for l in range(DEPTH):
        wpool_bd = jax.scipy.linalg.block_diag(*[w_pool[l, g] for g in range(len(POOL_WINDOWS))])
        h = _mixer_call(l, h, hb, w_in[l].astype(bf16), w_conv[l].T, wpool_bd.astype(bf16),
                        pool_scale[l][None], hg_norm_g[l][None], w_o[l].astype(bf16),
                        ln1_g[l][None], ln1_b[l][None])
        h = _ffn_call(l, h, _chunk_cols(w_up[l]).astype(bf16), _chunk_cols(w_ffn_conv[l].T),
                      _chunk_cols(b_ffn_conv[l][None]), w_down[l].astype(bf16),
                      ln2_g[l][None], ln2_b[l][None])
    return h[:, N_META:]
```

```python
import functools

import jax
import jax.numpy as jnp
from jax import lax
from jax.experimental import pallas as pl
from jax.experimental.pallas import tpu as pltpu

D_MODEL = 1024
N_META = 16
CONV_WIDTH = 256
HG_WIDTH = 512
HG_HEAD_DIM = 128
HG_HEADS = 4
POOL_WIDTH = 256
POOL_WINDOWS = (2, 4, 8, 16)
POOL_GROUP_DIM = 64
D_FF = 2816
D_IN = 3 * CONV_WIDTH + 4 * HG_WIDTH + POOL_WIDTH
DEPTH = 2
ALPHA = (2 * DEPTH) ** 0.25
LN_EPS = 1e-5
RMS_EPS = 1e-6
F_FLOOR = 1e-30
NEG_BIG = -1e30
LOG2_E = 1.4426950408889634

ROW_TILE = 432
CHUNK = 16
HALF = 8
CHUNK_UNROLL = 3
FF_CHUNK = 256
FF_GROUP = 4
N_FF_CHUNKS = D_FF // FF_CHUNK
VMEM_LIMIT_BYTES = 56 * 1024 * 1024

_C_CONV = 0
_C_HG = 3 * CONV_WIDTH
_C_POOL = _C_HG + 4 * HG_WIDTH


def _layer_norm(r, g, b):
    mu = jnp.mean(r, axis=-1, keepdims=True)
    c = r - mu
    var = jnp.mean(c * c, axis=-1, keepdims=True)
    return c * lax.rsqrt(var + LN_EPS) * g + b


def _sigmoid(x):
    return 1.0 / (1.0 + jnp.exp(-x))


def _shifted_rows(u, prev8, k):
    hist = jnp.concatenate([prev8, u[0:8]], axis=0)
    head = pltpu.roll(hist, k, 0)[8:16]
    return jnp.concatenate([head, pltpu.roll(u, k, 0)[8:]], axis=0)


def _mixer_kernel(layer, x_ref, hb_ref, win_ref, wconv_ref, wpool_ref, pscale_ref, gnorm_ref,
                  wo_ref, lng_ref, lnb_ref, out_ref,
                  zcarry, vcarry, st_ref, q_s, k_s, i_s, g_s, o_s, ymix, mix_s):
    f32, bf16 = jnp.float32, jnp.bfloat16
    tm = ROW_TILE
    nc = tm // CHUNK
    step = pl.program_id(1)

    @pl.when(step == 0)
    def _():
        zcarry[...] = jnp.zeros(zcarry.shape, f32)
        vcarry[...] = jnp.zeros(vcarry.shape, f32)
        st_ref[...] = jnp.zeros(st_ref.shape, f32)

    x = x_ref[0]
    xb = x.astype(bf16)

    hh = jnp.dot(xb, win_ref[:, _C_HG:_C_POOL], preferred_element_type=f32)
    hq = hh[:, 0:HG_WIDTH]
    fz = hh[:, HG_WIDTH:2 * HG_WIDTH]
    hi = hh[:, 2 * HG_WIDTH:3 * HG_WIDTH]
    gz = hh[:, 3 * HG_WIDTH:4 * HG_WIDTH]

    hb = hb_ref[...]
    e = jnp.exp(hb - jnp.max(hb, axis=0, keepdims=True))
    p = e / jnp.sum(e, axis=0, keepdims=True)
    lb = jnp.sum(p[0:layer + 1], axis=0, keepdims=True) - p[0:1]

    sig = _sigmoid(fz)
    f = lb + (1.0 - lb) * sig
    g = (jnp.log(jnp.maximum(f, F_FLOOR)) * LOG2_E).reshape(tm // HALF, HALF, HG_WIDTH)
    rid3 = lax.broadcasted_iota(jnp.int32, g.shape, 1)
    sh = 1
    while sh < HALF:
        g = g + jnp.where(rid3 >= sh, pltpu.roll(g, sh, 1), 0.0)
        sh *= 2
    g = g.reshape(nc, 2, HALF, HG_WIDTH)
    g_top = g[:, 0]
    g_s[:, 0:HALF, :] = g_top
    g_s[:, HALF:CHUNK, :] = g[:, 1] + g_top[:, HALF - 1:HALF, :]
    q_s[...] = (hq * (HG_HEAD_DIM ** -0.5)).reshape(nc, CHUNK, HG_WIDTH)
    k_s[...] = ((1.0 - lb) * (1.0 - sig)).reshape(nc, CHUNK, HG_WIDTH)
    i_s[...] = hi.reshape(nc, CHUNK, HG_WIDTH)

    hc = jnp.dot(xb, win_ref[:, _C_CONV:_C_HG], preferred_element_type=f32)
    cb = hc[:, 0:CONV_WIDTH]
    z = hc[:, CONV_WIDTH:2 * CONV_WIDTH] * hc[:, 2 * CONV_WIDTH:3 * CONV_WIDTH]
    zprev = zcarry[...]
    zcarry[...] = z[tm - 8:tm]
    wc = wconv_ref[...]
    conv = wc[0:1] * _shifted_rows(z, zprev, 2) + wc[1:2] * _shifted_rows(z, zprev, 1) + wc[2:3] * z
    ymix[:, 0:CONV_WIDTH] = (cb * conv).astype(bf16)

    pv = jnp.dot(xb, win_ref[:, _C_POOL:D_IN], preferred_element_type=f32)
    ext = jnp.concatenate([vcarry[...], pv], axis=0)
    vcarry[...] = pv[tm - 16:tm]
    s2 = ext + pltpu.roll(ext, 1, 0)
    s4 = s2 + pltpu.roll(s2, 2, 0)
    s8 = s4 + pltpu.roll(s4, 4, 0)
    s16 = s8 + pltpu.roll(s8, 8, 0)
    lane = lax.broadcasted_iota(jnp.int32, (1, POOL_WIDTH), 1)
    g0, g1, g2 = lane < POOL_GROUP_DIM, lane < 2 * POOL_GROUP_DIM, lane < 3 * POOL_GROUP_DIM
    win = jnp.where(g0, 2.0, jnp.where(g1, 4.0, jnp.where(g2, 8.0, 16.0))).astype(f32)
    wsum = jnp.where(g0, s2[16:], jnp.where(g1, s4[16:], jnp.where(g2, s8[16:], s16[16:])))
    pos = (step * tm + lax.broadcasted_iota(jnp.int32, (tm, 1), 0)).astype(f32)
    count = jnp.minimum(pos + 1.0, win)
    dpool = wsum / count - pv
    ypool = jnp.dot(dpool.astype(bf16), wpool_ref[...], preferred_element_type=f32) * pscale_ref[...]
    ymix[:, CONV_WIDTH + HG_WIDTH:D_MODEL] = ypool.astype(bf16)

    mix_s[...] = (jnp.dot(ymix[:, 0:CONV_WIDTH], wo_ref[0:CONV_WIDTH, :], preferred_element_type=f32)
                  + jnp.dot(ymix[:, CONV_WIDTH + HG_WIDTH:D_MODEL], wo_ref[CONV_WIDTH + HG_WIDTH:D_MODEL, :],
                            preferred_element_type=f32))

    rid = lax.broadcasted_iota(jnp.int32, (HALF, 1), 0)

    def chunk_body(c, carry):
        for h in range(HG_HEADS):
            ls = slice(h * HG_HEAD_DIM, (h + 1) * HG_HEAD_DIM)
            gc = g_s[c, :, ls]
            qc = q_s[c, :, ls]
            kc = k_s[c, :, ls]
            ic = i_s[c, :, ls]
            gl = gc[CHUNK - 1:CHUNK]
            st = st_ref[h]
            qt = (qc * jnp.exp2(gc)).astype(bf16)
            o_inter = lax.dot_general(qt, st.astype(bf16), (((1,), (1,)), ((), ())),
                                      preferred_element_type=f32)
            q_top, q_bot = qc[0:HALF], qc[HALF:CHUNK]
            g_top, g_bot = gc[0:HALF], gc[HALF:CHUNK]
            o_top = jnp.zeros((HALF, HG_HEAD_DIM), f32)
            o_bot = jnp.zeros((HALF, HG_HEAD_DIM), f32)
            for s in range(CHUNK):
                ks = k_s[c, s:s + 1, ls]
                gs = g_s[c, s:s + 1, ls]
                is_ = i_s[c, s:s + 1, ls]
                if s < HALF:
                    d_top = jnp.where(rid >= s, g_top - gs, NEG_BIG)
                    a_top = jnp.sum(q_top * ks * jnp.exp2(d_top), axis=-1, keepdims=True)
                    o_top = o_top + a_top * is_
                    d_bot = g_bot - gs
                else:
                    d_bot = jnp.where(rid >= s - HALF, g_bot - gs, NEG_BIG)
                a_bot = jnp.sum(q_bot * ks * jnp.exp2(d_bot), axis=-1, keepdims=True)
                o_bot = o_bot + a_bot * is_
            o_s[c, 0:HALF, ls] = o_inter[0:HALF] + o_top
            o_s[c, HALF:CHUNK, ls] = o_inter[HALF:CHUNK] + o_bot
            kt = (kc * jnp.exp2(gl - gc)).astype(bf16)
            delta = lax.dot_general(ic.astype(bf16), kt, (((0,), (0,)), ((), ())),
                                    preferred_element_type=f32)
            st_ref[h] = jnp.exp2(gl) * st + delta
        return carry

    lax.fori_loop(0, nc, chunk_body, 0, unroll=CHUNK_UNROLL)

    gn = gnorm_ref[...]
    for h in range(HG_HEADS):
        ls = slice(h * HG_HEAD_DIM, (h + 1) * HG_HEAD_DIM)
        o = o_s[:, :, ls].reshape(tm, HG_HEAD_DIM)
        o = o * lax.rsqrt(jnp.mean(o * o, axis=-1, keepdims=True) + RMS_EPS) * gn
        gate = gz[:, ls]
        o = o * (gate * _sigmoid(gate))
        ymix[:, CONV_WIDTH + h * HG_HEAD_DIM:CONV_WIDTH + (h + 1) * HG_HEAD_DIM] = o.astype(bf16)

    mix = mix_s[...] + jnp.dot(ymix[:, CONV_WIDTH:CONV_WIDTH + HG_WIDTH], wo_ref[CONV_WIDTH:CONV_WIDTH + HG_WIDTH, :],
                               preferred_element_type=f32)
    out_ref[0] = _layer_norm(ALPHA * x_ref[0] + mix, lng_ref[...], lnb_ref[...])


def _ffn_kernel(x_ref, wup_ref, wcv_ref, bcv_ref, wdn_ref, lng_ref, lnb_ref, out_ref,
                xb_s, ubuf, ucarry, a_s, acc):
    f32, bf16 = jnp.float32, jnp.bfloat16
    tm = ROW_TILE
    step = pl.program_id(1)

    @pl.when(step == 0)
    def _():
        ucarry[...] = jnp.zeros(ucarry.shape, f32)

    xb_s[...] = x_ref[0].astype(bf16)

    def up_project(j):
        ubuf[j % 2] = jnp.dot(xb_s[...], wup_ref[j], preferred_element_type=f32)

    up_project(0)
    partial = None
    for j in range(N_FF_CHUNKS):
        if j + 1 < N_FF_CHUNKS:
            up_project(j + 1)
        u = ubuf[j % 2]
        prev8 = ucarry[j]
        ucarry[j] = u[tm - 8:tm]
        w = wcv_ref[j]
        y = w[0:1] * _shifted_rows(u, prev8, 2) + w[1:2] * _shifted_rows(u, prev8, 1) + w[2:3] * u + bcv_ref[j]
        gate = y[:, 0:FF_CHUNK]
        val = y[:, FF_CHUNK:2 * FF_CHUNK]
        grp, k = divmod(j, FF_GROUP)
        a_g = a_s.at[grp % 2]
        a_g[:, k * FF_CHUNK:(k + 1) * FF_CHUNK] = (gate * _sigmoid(gate) * val).astype(bf16)
        if k == FF_GROUP - 1 or j == N_FF_CHUNKS - 1:
            j0 = j - k
            partial = jnp.dot(a_g[:, 0:(k + 1) * FF_CHUNK], wdn_ref[j0 * FF_CHUNK:(j + 1) * FF_CHUNK, :],
                              preferred_element_type=f32)
            if j == N_FF_CHUNKS - 1:
                break
            if j0 == 0:
                acc[...] = partial
            else:
                acc[...] += partial
    out_ref[0] = _layer_norm(ALPHA * x_ref[0] + acc[...] + partial, lng_ref[...], lnb_ref[...])


def _const_spec(shape):
    zeros = (0,) * len(shape)
    return pl.BlockSpec(shape, lambda b, i: zeros, pipeline_mode=pl.Buffered(1))


def _row_spec():
    return pl.BlockSpec((1, ROW_TILE, D_MODEL), lambda b, i: (b, i, 0))


def _compiler_params():
    return pltpu.CompilerParams(dimension_semantics=("arbitrary", "arbitrary"),
                                vmem_limit_bytes=VMEM_LIMIT_BYTES)


def _mixer_call(layer, h, hb, win, wconv, wpool, pscale, gnorm, wo, lng, lnb):
    batch, rows, _ = h.shape
    tm = ROW_TILE
    nc = tm // CHUNK
    f32 = jnp.float32
    return pl.pallas_call(
        functools.partial(_mixer_kernel, layer),
        grid=(batch, rows // tm),
        in_specs=[_row_spec(), _const_spec(hb.shape), _const_spec(win.shape), _const_spec(wconv.shape),
                  _const_spec(wpool.shape), _const_spec(pscale.shape), _const_spec(gnorm.shape),
                  _const_spec(wo.shape), _const_spec(lng.shape), _const_spec(lnb.shape)],
        out_specs=_row_spec(),
        out_shape=jax.ShapeDtypeStruct(h.shape, f32),
        scratch_shapes=[
            pltpu.VMEM((8, CONV_WIDTH), f32),
            pltpu.VMEM((16, POOL_WIDTH), f32),
            pltpu.VMEM((HG_HEADS, HG_HEAD_DIM, HG_HEAD_DIM), f32),
            pltpu.VMEM((nc, CHUNK, HG_WIDTH), f32),
            pltpu.VMEM((nc, CHUNK, HG_WIDTH), f32),
            pltpu.VMEM((nc, CHUNK, HG_WIDTH), f32),
            pltpu.VMEM((nc, CHUNK, HG_WIDTH), f32),
            pltpu.VMEM((nc, CHUNK, HG_WIDTH), f32),
            pltpu.VMEM((tm, D_MODEL), jnp.bfloat16),
            pltpu.VMEM((tm, D_MODEL), f32),
        ],
        compiler_params=_compiler_params(),
        name=f"mixer_l{layer}",
    )(h, hb, win, wconv, wpool, pscale, gnorm, wo, lng, lnb)


def _ffn_call(layer, h, wup, wcv, bcv, wdn, lng, lnb):
    batch, rows, _ = h.shape
    tm = ROW_TILE
    f32 = jnp.float32
    return pl.pallas_call(
        _ffn_kernel,
        grid=(batch, rows // tm),
        in_specs=[_row_spec(), _const_spec(wup.shape), _const_spec(wcv.shape), _const_spec(bcv.shape),
                  _const_spec(wdn.shape), _const_spec(lng.shape), _const_spec(lnb.shape)],
        out_specs=_row_spec(),
        out_shape=jax.ShapeDtypeStruct(h.shape, f32),
        scratch_shapes=[
            pltpu.VMEM((tm, D_MODEL), jnp.bfloat16),
            pltpu.VMEM((2, tm, 2 * FF_CHUNK), f32),
            pltpu.VMEM((N_FF_CHUNKS, 8, 2 * FF_CHUNK), f32),
            pltpu.VMEM((2, tm, FF_GROUP * FF_CHUNK), jnp.bfloat16),
            pltpu.VMEM((tm, D_MODEL), f32),
        ],
        compiler_params=_compiler_params(),
        name=f"ffn_l{layer}",
    )(h, wup, wcv, bcv, wdn, lng, lnb)


def _chunk_cols(a):
    lead = a.shape[:-1]
    a = a.reshape(lead + (2, N_FF_CHUNKS, FF_CHUNK))
    a = jnp.moveaxis(a, -2, 0)
    return a.reshape((N_FF_CHUNKS,) + lead + (2 * FF_CHUNK,))


def kernel(x, meta_tokens, hg_lower_bounds, w_in, w_conv, w_pool, pool_scale, hg_norm_g, w_o, ln1_g, ln1_b,
           w_up, w_ffn_conv, b_ffn_conv, w_down, ln2_g, ln2_b):
    bf16 = jnp.bfloat16
    batch = x.shape[0]
    assert (N_META + x.shape[1]) % ROW_TILE == 0 and ROW_TILE % (CHUNK * CHUNK_UNROLL) == 0
    meta = jnp.broadcast_to(meta_tokens[None].astype(x.dtype), (batch, N_META, D_MODEL))
    h = jnp.concatenate([meta, x], axis=1)
    hb = hg_lower_bounds.astype(jnp.float32)
    for l in range(DEPTH):
        wpool_bd = jax.scipy.linalg.block_diag(*[w_pool[l, g] for g in range(len(POOL_WINDOWS))])
        h = _mixer_call(l, h, hb, w_in[l].astype(bf16), w_conv[l].T, wpool_bd.astype(bf16),
                        pool_scale[l][None], hg_norm_g[l][None], w_o[l].astype(bf16),
                        ln1_g[l][None], ln1_b[l][None])
        h = _ffn_call(l, h, _chunk_cols(w_up[l]).astype(bf16), _chunk_cols(w_ffn_conv[l].T),
                      _chunk_cols(b_ffn_conv[l][None]), w_down[l].astype(bf16),
                      ln2_g[l][None], ln2_b[l][None])
    return h[:, N_META:]
```

```python
import functools

import jax
import jax.numpy as jnp
from jax import lax
from jax.experimental import pallas as pl
from jax.experimental.pallas import tpu as pltpu

D_MODEL = 1024
N_META = 16
CONV_WIDTH = 256
HG_WIDTH = 512
HG_HEAD_DIM = 128
HG_HEADS = 4
POOL_WIDTH = 256
POOL_WINDOWS = (2, 4, 8, 16)
POOL_GROUP_DIM = 64
D_FF = 2816
D_IN = 3 * CONV_WIDTH + 4 * HG_WIDTH + POOL_WIDTH
DEPTH = 2
ALPHA = (2 * DEPTH) ** 0.25
LN_EPS = 1e-5
RMS_EPS = 1e-6
F_FLOOR = 1e-30
NEG_BIG = -1e30
LOG2_E = 1.4426950408889634

ROW_TILE = 512
CHUNK = 16
HALF = 8
CHUNK_UNROLL = 4
FF_CHUNK = 256
FF_GROUP = 4
N_FF_CHUNKS = D_FF // FF_CHUNK
VMEM_LIMIT_BYTES = 56 * 1024 * 1024

_C_CONV = 0
_C_HG = 3 * CONV_WIDTH
_C_POOL = _C_HG + 4 * HG_WIDTH


def _layer_norm(r, g, b):
    mu = jnp.mean(r, axis=-1, keepdims=True)
    c = r - mu
    var = jnp.mean(c * c, axis=-1, keepdims=True)
    return c * lax.rsqrt(var + LN_EPS) * g + b


def _sigmoid(x):
    return 1.0 / (1.0 + jnp.exp(-x))


def _shifted_rows(u, prev8, k):
    hist = jnp.concatenate([prev8, u[0:8]], axis=0)
    head = pltpu.roll(hist, k, 0)[8:16]
    if u.shape[0] == 8:
        return head
    return jnp.concatenate([head, pltpu.roll(u, k, 0)[8:]], axis=0)


def _mixer_rows(layer, rows, is_meta, x, w, s):
    f32, bf16 = jnp.float32, jnp.bfloat16
    nc = rows // CHUNK
    xb = x.astype(bf16)

    hh = jnp.dot(xb, w.win[:, _C_HG:_C_POOL], preferred_element_type=f32)
    hq = hh[:, 0:HG_WIDTH]
    fz = hh[:, HG_WIDTH:2 * HG_WIDTH]
    hi = hh[:, 2 * HG_WIDTH:3 * HG_WIDTH]
    gz = hh[:, 3 * HG_WIDTH:4 * HG_WIDTH]

    hb = w.hb[...]
    e = jnp.exp(hb - jnp.max(hb, axis=0, keepdims=True))
    p = e / jnp.sum(e, axis=0, keepdims=True)
    lb = jnp.sum(p[0:layer + 1], axis=0, keepdims=True) - p[0:1]

    sig = _sigmoid(fz)
    f = lb + (1.0 - lb) * sig
    g = (jnp.log(jnp.maximum(f, F_FLOOR)) * LOG2_E).reshape(rows // HALF, HALF, HG_WIDTH)
    rid3 = lax.broadcasted_iota(jnp.int32, g.shape, 1)
    sh = 1
    while sh < HALF:
        g = g + jnp.where(rid3 >= sh, pltpu.roll(g, sh, 1), 0.0)
        sh *= 2
    g = g.reshape(nc, 2, HALF, HG_WIDTH)
    g_top = g[:, 0]
    s.g[0:nc, 0:HALF, :] = g_top
    s.g[0:nc, HALF:CHUNK, :] = g[:, 1] + g_top[:, HALF - 1:HALF, :]
    s.q[0:nc] = (hq * (HG_HEAD_DIM ** -0.5)).reshape(nc, CHUNK, HG_WIDTH)
    s.k[0:nc] = ((1.0 - lb) * (1.0 - sig)).reshape(nc, CHUNK, HG_WIDTH)
    s.i[0:nc] = hi.reshape(nc, CHUNK, HG_WIDTH)

    hc = jnp.dot(xb, w.win[:, _C_CONV:_C_HG], preferred_element_type=f32)
    cb = hc[:, 0:CONV_WIDTH]
    z = hc[:, CONV_WIDTH:2 * CONV_WIDTH] * hc[:, 2 * CONV_WIDTH:3 * CONV_WIDTH]
    zprev = s.zcarry[...]
    s.zcarry[...] = z[rows - 8:rows]
    wc = w.wconv[...]
    conv = wc[0:1] * _shifted_rows(z, zprev, 2) + wc[1:2] * _shifted_rows(z, zprev, 1) + wc[2:3] * z
    s.ymix[0:rows, 0:CONV_WIDTH] = (cb * conv).astype(bf16)

    pv = jnp.dot(xb, w.win[:, _C_POOL:D_IN], preferred_element_type=f32)
    ext = jnp.concatenate([s.vcarry[...], pv], axis=0)
    s.vcarry[...] = pv[rows - 16:rows]
    s2 = ext + pltpu.roll(ext, 1, 0)
    s4 = s2 + pltpu.roll(s2, 2, 0)
    s8 = s4 + pltpu.roll(s4, 4, 0)
    s16 = s8 + pltpu.roll(s8, 8, 0)
    lane = lax.broadcasted_iota(jnp.int32, (1, POOL_WIDTH), 1)
    g0, g1, g2 = lane < POOL_GROUP_DIM, lane < 2 * POOL_GROUP_DIM, lane < 3 * POOL_GROUP_DIM
    win = jnp.where(g0, 2.0, jnp.where(g1, 4.0, jnp.where(g2, 8.0, 16.0))).astype(f32)
    wsum = jnp.where(g0, s2[16:], jnp.where(g1, s4[16:], jnp.where(g2, s8[16:], s16[16:])))
    if is_meta:
        pos = lax.broadcasted_iota(jnp.int32, (rows, 1), 0).astype(f32)
        count = jnp.minimum(pos + 1.0, win)
    else:
        count = win
    dpool = wsum / count - pv
    ypool = jnp.dot(dpool.astype(bf16), w.wpool[...], preferred_element_type=f32) * w.pscale[...]
    s.ymix[0:rows, CONV_WIDTH + HG_WIDTH:D_MODEL] = ypool.astype(bf16)

    s.mix[0:rows] = (
        jnp.dot(s.ymix[0:rows, 0:CONV_WIDTH], w.wo[0:CONV_WIDTH, :], preferred_element_type=f32)
        + jnp.dot(s.ymix[0:rows, CONV_WIDTH + HG_WIDTH:D_MODEL], w.wo[CONV_WIDTH + HG_WIDTH:D_MODEL, :],
                  preferred_element_type=f32))

    rid = lax.broadcasted_iota(jnp.int32, (HALF, 1), 0)

    def chunk_body(c, carry):
        for h in range(HG_HEADS):
            ls = slice(h * HG_HEAD_DIM, (h + 1) * HG_HEAD_DIM)
            gc = s.g[c, :, ls]
            qc = s.q[c, :, ls]
            kc = s.k[c, :, ls]
            ic = s.i[c, :, ls]
            gl = gc[CHUNK - 1:CHUNK]
            st = s.st[h]
            qt = (qc * jnp.exp2(gc)).astype(bf16)
            o_inter = lax.dot_general(qt, st.astype(bf16), (((1,), (1,)), ((), ())),
                                      preferred_element_type=f32)
            q_top, q_bot = qc[0:HALF], qc[HALF:CHUNK]
            g_top, g_bot = gc[0:HALF], gc[HALF:CHUNK]
            o_top = jnp.zeros((HALF, HG_HEAD_DIM), f32)
            o_bot = jnp.zeros((HALF, HG_HEAD_DIM), f32)
            for t in range(CHUNK):
                ks = s.k[c, t:t + 1, ls]
                gs = s.g[c, t:t + 1, ls]
                is_ = s.i[c, t:t + 1, ls]
                if t < HALF:
                    d_top = jnp.where(rid >= t, g_top - gs, NEG_BIG)
                    a_top = jnp.sum(q_top * ks * jnp.exp2(d_top), axis=-1, keepdims=True)
                    o_top = o_top + a_top * is_
                    d_bot = g_bot - gs
                else:
                    d_bot = jnp.where(rid >= t - HALF, g_bot - gs, NEG_BIG)
                a_bot = jnp.sum(q_bot * ks * jnp.exp2(d_bot), axis=-1, keepdims=True)
                o_bot = o_bot + a_bot * is_
            s.o[c, 0:HALF, ls] = o_inter[0:HALF] + o_top
            s.o[c, HALF:CHUNK, ls] = o_inter[HALF:CHUNK] + o_bot
            kt = (kc * jnp.exp2(gl - gc)).astype(bf16)
            delta = lax.dot_general(ic.astype(bf16), kt, (((0,), (0,)), ((), ())),
                                    preferred_element_type=f32)
            s.st[h] = jnp.exp2(gl) * st + delta
        return carry

    if nc == 1:
        chunk_body(0, 0)
    else:
        lax.fori_loop(0, nc, chunk_body, 0, unroll=CHUNK_UNROLL)

    gn = w.gnorm[...]
    for h in range(HG_HEADS):
        ls = slice(h * HG_HEAD_DIM, (h + 1) * HG_HEAD_DIM)
        o = s.o[0:nc, :, ls].reshape(rows, HG_HEAD_DIM)
        o = o * lax.rsqrt(jnp.mean(o * o, axis=-1, keepdims=True) + RMS_EPS) * gn
        gate = gz[:, ls]
        o = o * (gate * _sigmoid(gate))
        s.ymix[0:rows, CONV_WIDTH + h * HG_HEAD_DIM:CONV_WIDTH + (h + 1) * HG_HEAD_DIM] = o.astype(bf16)

    mix = s.mix[0:rows] + jnp.dot(s.ymix[0:rows, CONV_WIDTH:CONV_WIDTH + HG_WIDTH],
                                  w.wo[CONV_WIDTH:CONV_WIDTH + HG_WIDTH, :], preferred_element_type=f32)
    return _layer_norm(ALPHA * x + mix, w.lng[...], w.lnb[...])


class _Refs:
    def __init__(self, **kw):
        self.__dict__.update(kw)


def _mixer_kernel(layer, x_ref, meta_ref, hb, win, wconv, wpool, pscale, gnorm, wo, lng, lnb,
                  out_ref, mout_ref,
                  zcarry, vcarry, st, q, k, i, g, o, ymix, mix):
    w = _Refs(hb=hb, win=win, wconv=wconv, wpool=wpool, pscale=pscale, gnorm=gnorm, wo=wo, lng=lng, lnb=lnb)
    s = _Refs(zcarry=zcarry, vcarry=vcarry, st=st, q=q, k=k, i=i, g=g, o=o, ymix=ymix, mix=mix)

    @pl.when(pl.program_id(1) == 0)
    def _():
        zcarry[...] = jnp.zeros(zcarry.shape, jnp.float32)
        vcarry[...] = jnp.zeros(vcarry.shape, jnp.float32)
        st[...] = jnp.zeros(st.shape, jnp.float32)
        mout_ref[0] = _mixer_rows(layer, N_META, True, meta_ref[0], w, s)

    out_ref[0] = _mixer_rows(layer, ROW_TILE, False, x_ref[0], w, s)


def _ffn_rows(rows, x, w, s):
    f32, bf16 = jnp.float32, jnp.bfloat16
    s.xb[0:rows] = x.astype(bf16)

    def up_project(j):
        for part in range(2):
            c0 = part * D_FF + j * FF_CHUNK
            s.u[j % 2, 0:rows, part * FF_CHUNK:(part + 1) * FF_CHUNK] = jnp.dot(
                s.xb[0:rows], w.wup[:, c0:c0 + FF_CHUNK], preferred_element_type=f32)

    def conv_rows(j, part):
        c0 = part * D_FF + j * FF_CHUNK
        u = s.u[j % 2, 0:rows, part * FF_CHUNK:(part + 1) * FF_CHUNK]
        prev8 = s.ucarry[j, :, part * FF_CHUNK:(part + 1) * FF_CHUNK]
        s.ucarry[j, :, part * FF_CHUNK:(part + 1) * FF_CHUNK] = u[rows - 8:rows]
        wc = w.wcv[:, c0:c0 + FF_CHUNK]
        return (wc[0:1] * _shifted_rows(u, prev8, 2) + wc[1:2] * _shifted_rows(u, prev8, 1) + wc[2:3] * u
                + w.bcv[:, c0:c0 + FF_CHUNK])

    up_project(0)
    partial = None
    for j in range(N_FF_CHUNKS):
        if j + 1 < N_FF_CHUNKS:
            up_project(j + 1)
        gate = conv_rows(j, 0)
        val = conv_rows(j, 1)
        grp, k = divmod(j, FF_GROUP)
        a_g = s.a.at[grp % 2]
        a_g[0:rows, k * FF_CHUNK:(k + 1) * FF_CHUNK] = (gate * _sigmoid(gate) * val).astype(bf16)
        if k == FF_GROUP - 1 or j == N_FF_CHUNKS - 1:
            j0 = j - k
            partial = jnp.dot(a_g[0:rows, 0:(k + 1) * FF_CHUNK], w.wdn[j0 * FF_CHUNK:(j + 1) * FF_CHUNK, :],
                              preferred_element_type=f32)
            if j == N_FF_CHUNKS - 1:
                break
            if j0 == 0:
                s.acc[0:rows] = partial
            else:
                s.acc[0:rows] += partial
    return _layer_norm(ALPHA * x + s.acc[0:rows] + partial, w.lng[...], w.lnb[...])


def _ffn_kernel(x_ref, meta_ref, wup, wcv, bcv, wdn, lng, lnb, out_ref, mout_ref,
                xb, u, ucarry, a, acc):
    w = _Refs(wup=wup, wcv=wcv, bcv=bcv, wdn=wdn, lng=lng, lnb=lnb)
    s = _Refs(xb=xb, u=u, ucarry=ucarry, a=a, acc=acc)

    @pl.when(pl.program_id(1) == 0)
    def _():
        ucarry[...] = jnp.zeros(ucarry.shape, jnp.float32)
        mout_ref[0] = _ffn_rows(N_META, meta_ref[0], w, s)

    out_ref[0] = _ffn_rows(ROW_TILE, x_ref[0], w, s)


def _const_spec(shape):
    zeros = (0,) * len(shape)
    return pl.BlockSpec(shape, lambda b, i: zeros, pipeline_mode=pl.Buffered(1))


def _row_spec():
    return pl.BlockSpec((1, ROW_TILE, D_MODEL), lambda b, i: (b, i, 0))


def _meta_spec():
    return pl.BlockSpec((1, N_META, D_MODEL), lambda b, i: (b, 0, 0))


def _compiler_params():
    return pltpu.CompilerParams(dimension_semantics=("arbitrary", "arbitrary"),
                                vmem_limit_bytes=VMEM_LIMIT_BYTES)


def _out_shapes(x, meta):
    return (jax.ShapeDtypeStruct(x.shape, jnp.float32), jax.ShapeDtypeStruct(meta.shape, jnp.float32))


def _mixer_call(layer, x, meta, *weights):
    batch, rows, _ = x.shape
    tm = ROW_TILE
    nc = tm // CHUNK
    f32 = jnp.float32
    return pl.pallas_call(
        functools.partial(_mixer_kernel, layer),
        grid=(batch, rows // tm),
        in_specs=[_row_spec(), _meta_spec()] + [_const_spec(a.shape) for a in weights],
        out_specs=(_row_spec(), _meta_spec()),
        out_shape=_out_shapes(x, meta),
        scratch_shapes=[
            pltpu.VMEM((8, CONV_WIDTH), f32),
            pltpu.VMEM((16, POOL_WIDTH), f32),
            pltpu.VMEM((HG_HEADS, HG_HEAD_DIM, HG_HEAD_DIM), f32),
            pltpu.VMEM((nc, CHUNK, HG_WIDTH), f32),
            pltpu.VMEM((nc, CHUNK, HG_WIDTH), f32),
            pltpu.VMEM((nc, CHUNK, HG_WIDTH), f32),
            pltpu.VMEM((nc, CHUNK, HG_WIDTH), f32),
            pltpu.VMEM((nc, CHUNK, HG_WIDTH), f32),
            pltpu.VMEM((tm, D_MODEL), jnp.bfloat16),
            pltpu.VMEM((tm, D_MODEL), f32),
        ],
        compiler_params=_compiler_params(),
        name=f"mixer_l{layer}",
    )(x, meta, *weights)


def _ffn_call(layer, x, meta, *weights):
    batch, rows, _ = x.shape
    tm = ROW_TILE
    f32 = jnp.float32
    return pl.pallas_call(
        _ffn_kernel,
        grid=(batch, rows // tm),
        in_specs=[_row_spec(), _meta_spec()] + [_const_spec(a.shape) for a in weights],
        out_specs=(_row_spec(), _meta_spec()),
        out_shape=_out_shapes(x, meta),
        scratch_shapes=[
            pltpu.VMEM((tm, D_MODEL), jnp.bfloat16),
            pltpu.VMEM((2, tm, 2 * FF_CHUNK), f32),
            pltpu.VMEM((N_FF_CHUNKS, 8, 2 * FF_CHUNK), f32),
            pltpu.VMEM((2, tm, FF_GROUP * FF_CHUNK), jnp.bfloat16),
            pltpu.VMEM((tm, D_MODEL), f32),
        ],
        compiler_params=_compiler_params(),
        name=f"ffn_l{layer}",
    )(x, meta, *weights)


def kernel(x, meta_tokens, hg_lower_bounds, w_in, w_conv, w_pool, pool_scale, hg_norm_g, w_o, ln1_g, ln1_b,
           w_up, w_ffn_conv, b_ffn_conv, w_down, ln2_g, ln2_b):
    bf16 = jnp.bfloat16
    batch, seq, _ = x.shape
    assert seq % ROW_TILE == 0 and ROW_TILE % (CHUNK * CHUNK_UNROLL) == 0
    h = x
    meta = jnp.broadcast_to(meta_tokens[None].astype(x.dtype), (batch, N_META, D_MODEL))
    hb = hg_lower_bounds.astype(jnp.float32)
    for l in range(DEPTH):
        wpool_bd = jax.scipy.linalg.block_diag(*[w_pool[l, g] for g in range(len(POOL_WINDOWS))])
        h, meta = _mixer_call(l, h, meta, hb, w_in[l].astype(bf16), w_conv[l].T, wpool_bd.astype(bf16),
                              pool_scale[l][None], hg_norm_g[l][None], w_o[l].astype(bf16),
                              ln1_g[l][None], ln1_b[l][None])
        h, meta = _ffn_call(l, h, meta, w_up[l].astype(bf16), w_ffn_conv[l].T, b_ffn_conv[l][None],
                            w_down[l].astype(bf16), ln2_g[l][None], ln2_b[l][None])
    return h
```

```python
import functools

import jax
import jax.numpy as jnp
from jax import lax
from jax.experimental import pallas as pl
from jax.experimental.pallas import tpu as pltpu

D_MODEL = 1024
N_META = 16
CONV_WIDTH = 256
HG_WIDTH = 512
HG_HEAD_DIM = 128
HG_HEADS = 4
POOL_WIDTH = 256
POOL_WINDOWS = (2, 4, 8, 16)
POOL_GROUP_DIM = 64
D_FF = 2816
D_IN = 3 * CONV_WIDTH + 4 * HG_WIDTH + POOL_WIDTH
DEPTH = 2
ALPHA = (2 * DEPTH) ** 0.25
LN_EPS = 1e-5
RMS_EPS = 1e-6
F_FLOOR = 1e-30
LOG2_E = 1.4426950408889634

ROW_TILE = 512
CHUNK = 32
HALF = 8
CHUNK_UNROLL = 4
FF_CHUNK = 256
FF_GROUP = 4
N_FF_CHUNKS = D_FF // FF_CHUNK
VMEM_LIMIT_BYTES = 56 * 1024 * 1024

_C_CONV = 0
_C_HG = 3 * CONV_WIDTH
_C_POOL = _C_HG + 4 * HG_WIDTH


def _layer_norm(r, g, b):
    mu = jnp.mean(r, axis=-1, keepdims=True)
    c = r - mu
    var = jnp.mean(c * c, axis=-1, keepdims=True)
    return c * lax.rsqrt(var + LN_EPS) * g + b


def _sigmoid(x):
    return 1.0 / (1.0 + jnp.exp(-x))


def _shifted_rows(u, prev8, k):
    hist = jnp.concatenate([prev8, u[0:8]], axis=0)
    head = pltpu.roll(hist, k, 0)[8:16]
    if u.shape[0] == 8:
        return head
    return jnp.concatenate([head, pltpu.roll(u, k, 0)[8:]], axis=0)


def _mixer_rows(layer, rows, is_meta, x, w, s):
    f32, bf16 = jnp.float32, jnp.bfloat16
    cl = min(CHUNK, rows)
    nc = rows // cl
    tiles = cl // HALF
    xb = x.astype(bf16)

    hh = jnp.dot(xb, w.win[:, _C_HG:_C_POOL], preferred_element_type=f32)
    hq = hh[:, 0:HG_WIDTH]
    fz = hh[:, HG_WIDTH:2 * HG_WIDTH]
    hi = hh[:, 2 * HG_WIDTH:3 * HG_WIDTH]
    gz = hh[:, 3 * HG_WIDTH:4 * HG_WIDTH]

    hb = w.hb[...]
    e = jnp.exp(hb - jnp.max(hb, axis=0, keepdims=True))
    p = e / jnp.sum(e, axis=0, keepdims=True)
    lb = jnp.sum(p[0:layer + 1], axis=0, keepdims=True) - p[0:1]

    sig = _sigmoid(fz)
    f = lb + (1.0 - lb) * sig
    g = (jnp.log(jnp.maximum(f, F_FLOOR)) * LOG2_E).reshape(rows // HALF, HALF, HG_WIDTH)
    rid3 = lax.broadcasted_iota(jnp.int32, g.shape, 1)
    sh = 1
    while sh < HALF:
        g = g + jnp.where(rid3 >= sh, pltpu.roll(g, sh, 1), 0.0)
        sh *= 2
    g = g.reshape(nc, tiles, HALF, HG_WIDTH)
    run = g[:, 0]
    s.g[0:nc, 0:HALF, :] = run
    for j in range(1, tiles):
        run = g[:, j] + run[:, HALF - 1:HALF, :]
        s.g[0:nc, j * HALF:(j + 1) * HALF, :] = run
    s.q[0:nc, 0:cl] = (hq * (HG_HEAD_DIM ** -0.5)).reshape(nc, cl, HG_WIDTH)
    s.k[0:nc, 0:cl] = ((1.0 - lb) * (1.0 - sig)).reshape(nc, cl, HG_WIDTH)
    s.i[0:nc, 0:cl] = hi.reshape(nc, cl, HG_WIDTH)

    hc = jnp.dot(xb, w.win[:, _C_CONV:_C_HG], preferred_element_type=f32)
    cb = hc[:, 0:CONV_WIDTH]
    z = hc[:, CONV_WIDTH:2 * CONV_WIDTH] * hc[:, 2 * CONV_WIDTH:3 * CONV_WIDTH]
    zprev = s.zcarry[...]
    s.zcarry[...] = z[rows - 8:rows]
    wc = w.wconv[...]
    conv = wc[0:1] * _shifted_rows(z, zprev, 2) + wc[1:2] * _shifted_rows(z, zprev, 1) + wc[2:3] * z
    s.ymix[0:rows, 0:CONV_WIDTH] = (cb * conv).astype(bf16)

    pv = jnp.dot(xb, w.win[:, _C_POOL:D_IN], preferred_element_type=f32)
    ext = jnp.concatenate([s.vcarry[...], pv], axis=0)
    s.vcarry[...] = pv[rows - 16:rows]
    s2 = ext + pltpu.roll(ext, 1, 0)
    s4 = s2 + pltpu.roll(s2, 2, 0)
    s8 = s4 + pltpu.roll(s4, 4, 0)
    s16 = s8 + pltpu.roll(s8, 8, 0)
    lane = lax.broadcasted_iota(jnp.int32, (1, POOL_WIDTH), 1)
    g0, g1, g2 = lane < POOL_GROUP_DIM, lane < 2 * POOL_GROUP_DIM, lane < 3 * POOL_GROUP_DIM
    win = jnp.where(g0, 2.0, jnp.where(g1, 4.0, jnp.where(g2, 8.0, 16.0))).astype(f32)
    wsum = jnp.where(g0, s2[16:], jnp.where(g1, s4[16:], jnp.where(g2, s8[16:], s16[16:])))
    if is_meta:
        pos = lax.broadcasted_iota(jnp.int32, (rows, 1), 0).astype(f32)
        count = jnp.minimum(pos + 1.0, win)
    else:
        count = win
    dpool = wsum / count - pv
    ypool = jnp.dot(dpool.astype(bf16), w.wpool[...], preferred_element_type=f32) * w.pscale[...]
    s.ymix[0:rows, CONV_WIDTH + HG_WIDTH:D_MODEL] = ypool.astype(bf16)

    s.mix[0:rows] = (
        jnp.dot(s.ymix[0:rows, 0:CONV_WIDTH], w.wo[0:CONV_WIDTH, :], preferred_element_type=f32)
        + jnp.dot(s.ymix[0:rows, CONV_WIDTH + HG_WIDTH:D_MODEL], w.wo[CONV_WIDTH + HG_WIDTH:D_MODEL, :],
                  preferred_element_type=f32))

    rid = lax.broadcasted_iota(jnp.int32, (HALF, 1), 0)
    sgn4 = jnp.where(rid < 4, -1.0, 1.0).astype(f32)
    sgn2 = jnp.where((rid & 2) == 0, -1.0, 1.0).astype(f32)
    odd = (rid & 1) == 1
    lo4 = rid < 4
    nst = HG_HEADS * cl
    tt = lax.broadcasted_iota(jnp.int32, (nst, nst), 0)
    uu = lax.broadcasted_iota(jnp.int32, (nst, nst), 1)
    levels = [m for m in (1, 2, 4, 8, 16, 32, 64) if m < cl]
    masks = [((tt // (2 * m)) == (uu // (2 * m))) & ((tt % (2 * m)) >= m) & ((uu % (2 * m)) < m)
             for m in levels]
    diag = tt == uu
    contract_lanes = (((1,), (1,)), ((), ()))
    head_lanes = [slice(h * HG_HEAD_DIM, (h + 1) * HG_HEAD_DIM) for h in range(HG_HEADS)]

    def stack_heads(v):
        return jnp.concatenate([v[:, ls] for ls in head_lanes], axis=0)

    def level_exponents(c, g_tiles, m):
        out = []
        for j, gt in enumerate(g_tiles):
            r0 = j * HALF
            if m >= HALF:
                ref = (r0 // (2 * m)) * 2 * m + m - 1
                gref = s.g[c, ref:ref + 1, :]
                out.append(gref - gt if (r0 % (2 * m)) < m else gt - gref)
            elif m == 4:
                out.append(sgn4 * (gt - s.g[c, r0 + 3:r0 + 4, :]))
            elif m == 2:
                out.append(sgn2 * (gt - jnp.where(lo4, s.g[c, r0 + 1:r0 + 2, :], s.g[c, r0 + 5:r0 + 6, :])))
            else:
                out.append(jnp.where(odd, gt - pltpu.roll(gt, 1, 0), 0.0))
        return jnp.concatenate(out, axis=0)

    def process_chunks(chunks):
        gcs = [s.g[c, 0:cl, :] for c in chunks]
        ibs = [s.i[c, 0:cl, :].astype(bf16) for c in chunks]
        states = [[s.st[h]] for h in range(HG_HEADS)]
        for n, c in enumerate(chunks):
            gl = gcs[n][cl - 1:cl]
            kt = (s.k[c, 0:cl, :] * jnp.exp2(gl - gcs[n])).astype(bf16)
            decay = jnp.exp2(gl)
            for h, ls in enumerate(head_lanes):
                delta = lax.dot_general(ibs[n][:, ls], kt[:, ls], (((0,), (0,)), ((), ())),
                                        preferred_element_type=f32)
                states[h].append(decay[:, ls] * states[h][-1] + delta)
        for h in range(HG_HEADS):
            s.st[h] = states[h][-1]
        amats = []
        for n, c in enumerate(chunks):
            g_tiles = [gcs[n][j * HALF:(j + 1) * HALF] for j in range(tiles)]
            qb = s.q[c, 0:cl, :].astype(bf16)
            kb = s.k[c, 0:cl, :].astype(bf16)
            a = jnp.where(diag, lax.dot_general(stack_heads(qb), stack_heads(kb), contract_lanes,
                                                preferred_element_type=f32), 0.0)
            for m, mask in zip(levels, masks):
                eb = jnp.exp2(level_exponents(c, g_tiles, m)).astype(bf16)
                a = jnp.where(mask, lax.dot_general(stack_heads(qb * eb), stack_heads(kb * eb), contract_lanes,
                                                    preferred_element_type=f32), a)
            amats.append(a.astype(bf16))
        inters = []
        for n, c in enumerate(chunks):
            qt = (s.q[c, 0:cl, :] * jnp.exp2(gcs[n])).astype(bf16)
            inters.append([lax.dot_general(qt[:, ls], states[h][n].astype(bf16), contract_lanes,
                                           preferred_element_type=f32) for h, ls in enumerate(head_lanes)])
        for n, c in enumerate(chunks):
            intra = jnp.dot(amats[n], stack_heads(ibs[n]), preferred_element_type=f32)
            for h, ls in enumerate(head_lanes):
                s.o[c, 0:cl, ls] = inters[n][h] + intra[h * cl:(h + 1) * cl]

    if nc == 1:
        process_chunks([0])
    else:
        def loop_body(it, carry):
            process_chunks([it * CHUNK_UNROLL + u for u in range(CHUNK_UNROLL)])
            return carry

        lax.fori_loop(0, nc // CHUNK_UNROLL, loop_body, 0)

    gn = w.gnorm[...]
    for h in range(HG_HEADS):
        ls = slice(h * HG_HEAD_DIM, (h + 1) * HG_HEAD_DIM)
        o = s.o[0:nc, 0:cl, ls].reshape(rows, HG_HEAD_DIM)
        o = o * lax.rsqrt(jnp.mean(o * o, axis=-1, keepdims=True) + RMS_EPS) * gn
        gate = gz[:, ls]
        o = o * (gate * _sigmoid(gate))
        s.ymix[0:rows, CONV_WIDTH + h * HG_HEAD_DIM:CONV_WIDTH + (h + 1) * HG_HEAD_DIM] = o.astype(bf16)

    mix = s.mix[0:rows] + jnp.dot(s.ymix[0:rows, CONV_WIDTH:CONV_WIDTH + HG_WIDTH],
                                  w.wo[CONV_WIDTH:CONV_WIDTH + HG_WIDTH, :], preferred_element_type=f32)
    return _layer_norm(ALPHA * x + mix, w.lng[...], w.lnb[...])


class _Refs:
    def __init__(self, **kw):
        self.__dict__.update(kw)


def _mixer_kernel(layer, x_ref, meta_ref, hb, win, wconv, wpool, pscale, gnorm, wo, lng, lnb,
                  out_ref, mout_ref,
                  zcarry, vcarry, st, q, k, i, g, o, ymix, mix):
    w = _Refs(hb=hb, win=win, wconv=wconv, wpool=wpool, pscale=pscale, gnorm=gnorm, wo=wo, lng=lng, lnb=lnb)
    s = _Refs(zcarry=zcarry, vcarry=vcarry, st=st, q=q, k=k, i=i, g=g, o=o, ymix=ymix, mix=mix)

    @pl.when(pl.program_id(1) == 0)
    def _():
        zcarry[...] = jnp.zeros(zcarry.shape, jnp.float32)
        vcarry[...] = jnp.zeros(vcarry.shape, jnp.float32)
        st[...] = jnp.zeros(st.shape, jnp.float32)
        mout_ref[0] = _mixer_rows(layer, N_META, True, meta_ref[0], w, s)

    out_ref[0] = _mixer_rows(layer, ROW_TILE, False, x_ref[0], w, s)


def _ffn_rows(rows, x, w, s):
    f32, bf16 = jnp.float32, jnp.bfloat16
    s.xb[0:rows] = x.astype(bf16)

    def up_project(j):
        for part in range(2):
            c0 = part * D_FF + j * FF_CHUNK
            s.u[j % 2, 0:rows, part * FF_CHUNK:(part + 1) * FF_CHUNK] = jnp.dot(
                s.xb[0:rows], w.wup[:, c0:c0 + FF_CHUNK], preferred_element_type=f32)

    def conv_rows(j, part):
        c0 = part * D_FF + j * FF_CHUNK
        u = s.u[j % 2, 0:rows, part * FF_CHUNK:(part + 1) * FF_CHUNK]
        prev8 = s.ucarry[j, :, part * FF_CHUNK:(part + 1) * FF_CHUNK]
        s.ucarry[j, :, part * FF_CHUNK:(part + 1) * FF_CHUNK] = u[rows - 8:rows]
        wc = w.wcv[:, c0:c0 + FF_CHUNK]
        return (wc[0:1] * _shifted_rows(u, prev8, 2) + wc[1:2] * _shifted_rows(u, prev8, 1) + wc[2:3] * u
                + w.bcv[:, c0:c0 + FF_CHUNK])

    up_project(0)
    partial = None
    for j in range(N_FF_CHUNKS):
        if j + 1 < N_FF_CHUNKS:
            up_project(j + 1)
        gate = conv_rows(j, 0)
        val = conv_rows(j, 1)
        grp, k = divmod(j, FF_GROUP)
        a_g = s.a.at[grp % 2]
        a_g[0:rows, k * FF_CHUNK:(k + 1) * FF_CHUNK] = (gate * _sigmoid(gate) * val).astype(bf16)
        if k == FF_GROUP - 1 or j == N_FF_CHUNKS - 1:
            j0 = j - k
            partial = jnp.dot(a_g[0:rows, 0:(k + 1) * FF_CHUNK], w.wdn[j0 * FF_CHUNK:(j + 1) * FF_CHUNK, :],
                              preferred_element_type=f32)
            if j == N_FF_CHUNKS - 1:
                break
            if j0 == 0:
                s.acc[0:rows] = partial
            else:
                s.acc[0:rows] += partial
    return _layer_norm(ALPHA * x + s.acc[0:rows] + partial, w.lng[...], w.lnb[...])


def _ffn_kernel(x_ref, meta_ref, wup, wcv, bcv, wdn, lng, lnb, out_ref, mout_ref,
                xb, u, ucarry, a, acc):
    w = _Refs(wup=wup, wcv=wcv, bcv=bcv, wdn=wdn, lng=lng, lnb=lnb)
    s = _Refs(xb=xb, u=u, ucarry=ucarry, a=a, acc=acc)

    @pl.when(pl.program_id(1) == 0)
    def _():
        ucarry[...] = jnp.zeros(ucarry.shape, jnp.float32)
        mout_ref[0] = _ffn_rows(N_META, meta_ref[0], w, s)

    out_ref[0] = _ffn_rows(ROW_TILE, x_ref[0], w, s)


def _const_spec(shape):
    zeros = (0,) * len(shape)
    return pl.BlockSpec(shape, lambda b, i: zeros, pipeline_mode=pl.Buffered(1))


def _row_spec():
    return pl.BlockSpec((1, ROW_TILE, D_MODEL), lambda b, i: (b, i, 0))


def _meta_spec():
    return pl.BlockSpec((1, N_META, D_MODEL), lambda b, i: (b, 0, 0))


def _compiler_params():
    return pltpu.CompilerParams(dimension_semantics=("arbitrary", "arbitrary"),
                                vmem_limit_bytes=VMEM_LIMIT_BYTES)


def _out_shapes(x, meta):
    return (jax.ShapeDtypeStruct(x.shape, jnp.float32), jax.ShapeDtypeStruct(meta.shape, jnp.float32))


def _mixer_call(layer, x, meta, *weights):
    batch, rows, _ = x.shape
    tm = ROW_TILE
    nc = tm // CHUNK
    f32 = jnp.float32
    return pl.pallas_call(
        functools.partial(_mixer_kernel, layer),
        grid=(batch, rows // tm),
        in_specs=[_row_spec(), _meta_spec()] + [_const_spec(a.shape) for a in weights],
        out_specs=(_row_spec(), _meta_spec()),
        out_shape=_out_shapes(x, meta),
        scratch_shapes=[
            pltpu.VMEM((8, CONV_WIDTH), f32),
            pltpu.VMEM((16, POOL_WIDTH), f32),
            pltpu.VMEM((HG_HEADS, HG_HEAD_DIM, HG_HEAD_DIM), f32),
            pltpu.VMEM((nc, CHUNK, HG_WIDTH), f32),
            pltpu.VMEM((nc, CHUNK, HG_WIDTH), f32),
            pltpu.VMEM((nc, CHUNK, HG_WIDTH), f32),
            pltpu.VMEM((nc, CHUNK, HG_WIDTH), f32),
            pltpu.VMEM((nc, CHUNK, HG_WIDTH), f32),
            pltpu.VMEM((tm, D_MODEL), jnp.bfloat16),
            pltpu.VMEM((tm, D_MODEL), f32),
        ],
        compiler_params=_compiler_params(),
        name=f"mixer_l{layer}",
    )(x, meta, *weights)


def _ffn_call(layer, x, meta, *weights):
    batch, rows, _ = x.shape
    tm = ROW_TILE
    f32 = jnp.float32
    return pl.pallas_call(
        _ffn_kernel,
        grid=(batch, rows // tm),
        in_specs=[_row_spec(), _meta_spec()] + [_const_spec(a.shape) for a in weights],
        out_specs=(_row_spec(), _meta_spec()),
        out_shape=_out_shapes(x, meta),
        scratch_shapes=[
            pltpu.VMEM((tm, D_MODEL), jnp.bfloat16),
            pltpu.VMEM((2, tm, 2 * FF_CHUNK), f32),
            pltpu.VMEM((N_FF_CHUNKS, 8, 2 * FF_CHUNK), f32),
            pltpu.VMEM((2, tm, FF_GROUP * FF_CHUNK), jnp.bfloat16),
            pltpu.VMEM((tm, D_MODEL), f32),
        ],
        compiler_params=_compiler_params(),
        name=f"ffn_l{layer}",
    )(x, meta, *weights)


def kernel(x, meta_tokens, hg_lower_bounds, w_in, w_conv, w_pool, pool_scale, hg_norm_g, w_o, ln1_g, ln1_b,
           w_up, w_ffn_conv, b_ffn_conv, w_down, ln2_g, ln2_b):
    bf16 = jnp.bfloat16
    batch, seq, _ = x.shape
    assert seq % ROW_TILE == 0 and ROW_TILE % (CHUNK * CHUNK_UNROLL) == 0
    h = x
    meta = jnp.broadcast_to(meta_tokens[None].astype(x.dtype), (batch, N_META, D_MODEL))
    hb = hg_lower_bounds.astype(jnp.float32)
    for l in range(DEPTH):
        wpool_bd = jax.scipy.linalg.block_diag(*[w_pool[l, g] for g in range(len(POOL_WINDOWS))])
        h, meta = _mixer_call(l, h, meta, hb, w_in[l].astype(bf16), w_conv[l].T, wpool_bd.astype(bf16),
                              pool_scale[l][None], hg_norm_g[l][None], w_o[l].astype(bf16),
                              ln1_g[l][None], ln1_b[l][None])
        h, meta = _ffn_call(l, h, meta, w_up[l].astype(bf16), w_ffn_conv[l].T, b_ffn_conv[l][None],
                            w_down[l].astype(bf16), ln2_g[l][None], ln2_b[l][None])
    return h
```

```python
import functools

import jax
import jax.numpy as jnp
from jax import lax
from jax.experimental import pallas as pl
from jax.experimental.pallas import tpu as pltpu

D_MODEL = 1024
N_META = 16
CONV_WIDTH = 256
HG_WIDTH = 512
HG_HEAD_DIM = 128
HG_HEADS = 4
POOL_WIDTH = 256
POOL_WINDOWS = (2, 4, 8, 16)
POOL_GROUP_DIM = 64
D_FF = 2816
D_IN = 3 * CONV_WIDTH + 4 * HG_WIDTH + POOL_WIDTH
DEPTH = 2
ALPHA = (2 * DEPTH) ** 0.25
LN_EPS = 1e-5
RMS_EPS = 1e-6
F_FLOOR = 1e-30
LOG2_E = 1.4426950408889634

ROW_TILE = 512
CHUNK = 32
HALF = 8
CHUNK_UNROLL = 4
PROJ_PIECE = 256
FF_CHUNK = 256
FF_GROUP = 4
N_FF_CHUNKS = D_FF // FF_CHUNK
VMEM_LIMIT_BYTES = 56 * 1024 * 1024

_C_CONV = 0
_C_HG = 3 * CONV_WIDTH
_C_POOL = _C_HG + 4 * HG_WIDTH


def _layer_norm(r, g, b):
    mu = jnp.mean(r, axis=-1, keepdims=True)
    c = r - mu
    var = jnp.mean(c * c, axis=-1, keepdims=True)
    return c * lax.rsqrt(var + LN_EPS) * g + b


def _sigmoid(x):
    return 1.0 / (1.0 + jnp.exp(-x))


def _shifted_rows(u, prev8, k):
    hist = jnp.concatenate([prev8, u[0:8]], axis=0)
    head = pltpu.roll(hist, k, 0)[8:16]
    if u.shape[0] == 8:
        return head
    return jnp.concatenate([head, pltpu.roll(u, k, 0)[8:]], axis=0)


def _mixer_rows(layer, rows, is_meta, x, proj, between, w, s):
    f32, bf16 = jnp.float32, jnp.bfloat16
    cl = min(CHUNK, rows)
    nc = rows // cl
    tiles = cl // HALF
    between = list(between)

    hh = proj(_C_HG, _C_POOL)
    hq = hh[:, 0:HG_WIDTH]
    fz = hh[:, HG_WIDTH:2 * HG_WIDTH]
    hi = hh[:, 2 * HG_WIDTH:3 * HG_WIDTH]
    gz = hh[:, 3 * HG_WIDTH:4 * HG_WIDTH]

    hb = w.hb[...]
    e = jnp.exp(hb - jnp.max(hb, axis=0, keepdims=True))
    p = e / jnp.sum(e, axis=0, keepdims=True)
    lb = jnp.sum(p[0:layer + 1], axis=0, keepdims=True) - p[0:1]

    sig = _sigmoid(fz)
    f = lb + (1.0 - lb) * sig
    g = (jnp.log(jnp.maximum(f, F_FLOOR)) * LOG2_E).reshape(rows // HALF, HALF, HG_WIDTH)
    rid3 = lax.broadcasted_iota(jnp.int32, g.shape, 1)
    sh = 1
    while sh < HALF:
        g = g + jnp.where(rid3 >= sh, pltpu.roll(g, sh, 1), 0.0)
        sh *= 2
    g = g.reshape(nc, tiles, HALF, HG_WIDTH)
    run = g[:, 0]
    s.g[0:nc, 0:HALF, :] = run
    for j in range(1, tiles):
        run = g[:, j] + run[:, HALF - 1:HALF, :]
        s.g[0:nc, j * HALF:(j + 1) * HALF, :] = run
    s.q[0:nc, 0:cl] = (hq * (HG_HEAD_DIM ** -0.5)).reshape(nc, cl, HG_WIDTH)
    s.k[0:nc, 0:cl] = ((1.0 - lb) * (1.0 - sig)).reshape(nc, cl, HG_WIDTH)
    s.i[0:nc, 0:cl] = hi.reshape(nc, cl, HG_WIDTH)
    s.gate[0:rows] = gz * _sigmoid(gz)

    hc = proj(_C_CONV, _C_HG)
    cb = hc[:, 0:CONV_WIDTH]
    z = hc[:, CONV_WIDTH:2 * CONV_WIDTH] * hc[:, 2 * CONV_WIDTH:3 * CONV_WIDTH]
    zprev = s.zcarry[...]
    s.zcarry[...] = z[rows - 8:rows]
    wc = w.wconv[...]
    conv = wc[0:1] * _shifted_rows(z, zprev, 2) + wc[1:2] * _shifted_rows(z, zprev, 1) + wc[2:3] * z
    s.ymix[0:rows, 0:CONV_WIDTH] = (cb * conv).astype(bf16)

    pv = proj(_C_POOL, D_IN)
    ext = jnp.concatenate([s.vcarry[...], pv], axis=0)
    s.vcarry[...] = pv[rows - 16:rows]
    s2 = ext + pltpu.roll(ext, 1, 0)
    s4 = s2 + pltpu.roll(s2, 2, 0)
    s8 = s4 + pltpu.roll(s4, 4, 0)
    s16 = s8 + pltpu.roll(s8, 8, 0)
    lane = lax.broadcasted_iota(jnp.int32, (1, POOL_WIDTH), 1)
    g0, g1, g2 = lane < POOL_GROUP_DIM, lane < 2 * POOL_GROUP_DIM, lane < 3 * POOL_GROUP_DIM
    win = jnp.where(g0, 2.0, jnp.where(g1, 4.0, jnp.where(g2, 8.0, 16.0))).astype(f32)
    wsum = jnp.where(g0, s2[16:], jnp.where(g1, s4[16:], jnp.where(g2, s8[16:], s16[16:])))
    if is_meta:
        pos = lax.broadcasted_iota(jnp.int32, (rows, 1), 0).astype(f32)
        count = jnp.minimum(pos + 1.0, win)
    else:
        count = win
    dpool = wsum / count - pv
    ypool = jnp.dot(dpool.astype(bf16), w.wpool[...], preferred_element_type=f32) * w.pscale[...]
    s.ymix[0:rows, CONV_WIDTH + HG_WIDTH:D_MODEL] = ypool.astype(bf16)

    s.mix[0:rows] = (
        jnp.dot(s.ymix[0:rows, 0:CONV_WIDTH], w.wo[0:CONV_WIDTH, :], preferred_element_type=f32)
        + jnp.dot(s.ymix[0:rows, CONV_WIDTH + HG_WIDTH:D_MODEL], w.wo[CONV_WIDTH + HG_WIDTH:D_MODEL, :],
                  preferred_element_type=f32))

    rid = lax.broadcasted_iota(jnp.int32, (HALF, 1), 0)
    sgn4 = jnp.where(rid < 4, -1.0, 1.0).astype(f32)
    sgn2 = jnp.where((rid & 2) == 0, -1.0, 1.0).astype(f32)
    odd = (rid & 1) == 1
    lo4 = rid < 4
    nst = HG_HEADS * cl
    tt = lax.broadcasted_iota(jnp.int32, (nst, nst), 0)
    uu = lax.broadcasted_iota(jnp.int32, (nst, nst), 1)
    levels = [m for m in (1, 2, 4, 8, 16, 32, 64) if m < cl]
    masks = [((tt // (2 * m)) == (uu // (2 * m))) & ((tt % (2 * m)) >= m) & ((uu % (2 * m)) < m)
             for m in levels]
    diag = tt == uu
    contract_lanes = (((1,), (1,)), ((), ()))
    head_lanes = [slice(h * HG_HEAD_DIM, (h + 1) * HG_HEAD_DIM) for h in range(HG_HEADS)]

    def stack_heads(v):
        return jnp.concatenate([v[:, ls] for ls in head_lanes], axis=0)

    def level_exponents(c, g_tiles, m):
        out = []
        for j, gt in enumerate(g_tiles):
            r0 = j * HALF
            if m >= HALF:
                ref = (r0 // (2 * m)) * 2 * m + m - 1
                gref = s.g[c, ref:ref + 1, :]
                out.append(gref - gt if (r0 % (2 * m)) < m else gt - gref)
            elif m == 4:
                out.append(sgn4 * (gt - s.g[c, r0 + 3:r0 + 4, :]))
            elif m == 2:
                out.append(sgn2 * (gt - jnp.where(lo4, s.g[c, r0 + 1:r0 + 2, :], s.g[c, r0 + 5:r0 + 6, :])))
            else:
                out.append(jnp.where(odd, gt - pltpu.roll(gt, 1, 0), 0.0))
        return jnp.concatenate(out, axis=0)

    def emit_between():
        if between:
            between.pop(0)()

    def process_chunks(chunks):
        gcs = [s.g[c, 0:cl, :] for c in chunks]
        ibs = [s.i[c, 0:cl, :].astype(bf16) for c in chunks]
        states = [[s.st[h]] for h in range(HG_HEADS)]
        for n, c in enumerate(chunks):
            gl = gcs[n][cl - 1:cl]
            kt = (s.k[c, 0:cl, :] * jnp.exp2(gl - gcs[n])).astype(bf16)
            decay = jnp.exp2(gl)
            for h, ls in enumerate(head_lanes):
                delta = lax.dot_general(ibs[n][:, ls], kt[:, ls], (((0,), (0,)), ((), ())),
                                        preferred_element_type=f32)
                states[h].append(decay[:, ls] * states[h][-1] + delta)
        for h in range(HG_HEADS):
            s.st[h] = states[h][-1]
        emit_between()
        amats = []
        for n, c in enumerate(chunks):
            g_tiles = [gcs[n][j * HALF:(j + 1) * HALF] for j in range(tiles)]
            qb = s.q[c, 0:cl, :].astype(bf16)
            kb = s.k[c, 0:cl, :].astype(bf16)
            a = jnp.where(diag, lax.dot_general(stack_heads(qb), stack_heads(kb), contract_lanes,
                                                preferred_element_type=f32), 0.0)
            for m, mask in zip(levels, masks):
                eb = jnp.exp2(level_exponents(c, g_tiles, m)).astype(bf16)
                a = jnp.where(mask, lax.dot_general(stack_heads(qb * eb), stack_heads(kb * eb), contract_lanes,
                                                    preferred_element_type=f32), a)
            amats.append(a.astype(bf16))
        emit_between()
        inters = []
        for n, c in enumerate(chunks):
            qt = (s.q[c, 0:cl, :] * jnp.exp2(gcs[n])).astype(bf16)
            inters.append([lax.dot_general(qt[:, ls], states[h][n].astype(bf16), contract_lanes,
                                           preferred_element_type=f32) for h, ls in enumerate(head_lanes)])
        emit_between()
        for n, c in enumerate(chunks):
            intra = jnp.dot(amats[n], stack_heads(ibs[n]), preferred_element_type=f32)
            for h, ls in enumerate(head_lanes):
                s.o[c, 0:cl, ls] = inters[n][h] + intra[h * cl:(h + 1) * cl]

    for c0 in range(0, nc, CHUNK_UNROLL):
        process_chunks(list(range(c0, min(c0 + CHUNK_UNROLL, nc))))
    while between:
        emit_between()

    gn = w.gnorm[...]
    for h in range(HG_HEADS):
        ls = slice(h * HG_HEAD_DIM, (h + 1) * HG_HEAD_DIM)
        o = s.o[0:nc, 0:cl, ls].reshape(rows, HG_HEAD_DIM)
        o = o * lax.rsqrt(jnp.mean(o * o, axis=-1, keepdims=True) + RMS_EPS) * gn
        o = o * s.gate[0:rows, ls]
        s.ymix[0:rows, CONV_WIDTH + h * HG_HEAD_DIM:CONV_WIDTH + (h + 1) * HG_HEAD_DIM] = o.astype(bf16)

    mix = s.mix[0:rows] + jnp.dot(s.ymix[0:rows, CONV_WIDTH:CONV_WIDTH + HG_WIDTH],
                                  w.wo[CONV_WIDTH:CONV_WIDTH + HG_WIDTH, :], preferred_element_type=f32)
    return _layer_norm(ALPHA * x + mix, w.lng[...], w.lnb[...])


class _Refs:
    def __init__(self, **kw):
        self.__dict__.update(kw)


def _mixer_kernel(layer, x_ref, xnext_ref, meta_ref, hb, win, wconv, wpool, pscale, gnorm, wo, lng, lnb,
                  out_ref, mout_ref,
                  zcarry, vcarry, st, q, k, i, g, o, gate, ymix, mix, hproj, xnb):
    f32, bf16 = jnp.float32, jnp.bfloat16
    w = _Refs(hb=hb, win=win, wconv=wconv, wpool=wpool, pscale=pscale, gnorm=gnorm, wo=wo, lng=lng, lnb=lnb)
    s = _Refs(zcarry=zcarry, vcarry=vcarry, st=st, q=q, k=k, i=i, g=g, o=o, gate=gate, ymix=ymix, mix=mix)
    first_of_batch = pl.program_id(1) == 0

    @pl.when(first_of_batch & (pl.program_id(0) == 0))
    def _():
        hproj[...] = jnp.dot(x_ref[0].astype(bf16), win[...], preferred_element_type=f32)

    @pl.when(first_of_batch)
    def _():
        zcarry[...] = jnp.zeros(zcarry.shape, f32)
        vcarry[...] = jnp.zeros(vcarry.shape, f32)
        st[...] = jnp.zeros(st.shape, f32)
        mb = meta_ref[0].astype(bf16)
        mout_ref[0] = _mixer_rows(
            layer, N_META, True, meta_ref[0],
            lambda c0, c1: jnp.dot(mb, win[:, c0:c1], preferred_element_type=f32), [], w, s)

    xnb[...] = xnext_ref[0].astype(bf16)

    def piece(c0):
        def run():
            hproj[:, c0:c0 + PROJ_PIECE] = jnp.dot(xnb[...], win[:, c0:c0 + PROJ_PIECE], preferred_element_type=f32)
        return run

    out_ref[0] = _mixer_rows(layer, ROW_TILE, False, x_ref[0], lambda c0, c1: hproj[:, c0:c1],
                             [piece(c0) for c0 in range(0, D_IN, PROJ_PIECE)], w, s)


def _ffn_rows(rows, x, w, s):
    f32, bf16 = jnp.float32, jnp.bfloat16
    s.xb[0:rows] = x.astype(bf16)

    def up_project(j):
        for part in range(2):
            c0 = part * D_FF + j * FF_CHUNK
            s.u[j % 2, 0:rows, part * FF_CHUNK:(part + 1) * FF_CHUNK] = jnp.dot(
                s.xb[0:rows], w.wup[:, c0:c0 + FF_CHUNK], preferred_element_type=f32)

    def conv_rows(j, part):
        c0 = part * D_FF + j * FF_CHUNK
        u = s.u[j % 2, 0:rows, part * FF_CHUNK:(part + 1) * FF_CHUNK]
        prev8 = s.ucarry[j, :, part * FF_CHUNK:(part + 1) * FF_CHUNK]
        s.ucarry[j, :, part * FF_CHUNK:(part + 1) * FF_CHUNK] = u[rows - 8:rows]
        wc = w.wcv[:, c0:c0 + FF_CHUNK]
        return (wc[0:1] * _shifted_rows(u, prev8, 2) + wc[1:2] * _shifted_rows(u, prev8, 1) + wc[2:3] * u
                + w.bcv[:, c0:c0 + FF_CHUNK])

    up_project(0)
    partial = None
    for j in range(N_FF_CHUNKS):
        if j + 1 < N_FF_CHUNKS:
            up_project(j + 1)
        gate = conv_rows(j, 0)
        val = conv_rows(j, 1)
        grp, k = divmod(j, FF_GROUP)
        a_g = s.a.at[grp % 2]
        a_g[0:rows, k * FF_CHUNK:(k + 1) * FF_CHUNK] = (gate * _sigmoid(gate) * val).astype(bf16)
        if k == FF_GROUP - 1 or j == N_FF_CHUNKS - 1:
            j0 = j - k
            partial = jnp.dot(a_g[0:rows, 0:(k + 1) * FF_CHUNK], w.wdn[j0 * FF_CHUNK:(j + 1) * FF_CHUNK, :],
                              preferred_element_type=f32)
            if j == N_FF_CHUNKS - 1:
                break
            if j0 == 0:
                s.acc[0:rows] = partial
            else:
                s.acc[0:rows] += partial
    return _layer_norm(ALPHA * x + s.acc[0:rows] + partial, w.lng[...], w.lnb[...])


def _ffn_kernel(x_ref, meta_ref, wup, wcv, bcv, wdn, lng, lnb, out_ref, mout_ref,
                xb, u, ucarry, a, acc):
    w = _Refs(wup=wup, wcv=wcv, bcv=bcv, wdn=wdn, lng=lng, lnb=lnb)
    s = _Refs(xb=xb, u=u, ucarry=ucarry, a=a, acc=acc)

    @pl.when(pl.program_id(1) == 0)
    def _():
        ucarry[...] = jnp.zeros(ucarry.shape, jnp.float32)
        mout_ref[0] = _ffn_rows(N_META, meta_ref[0], w, s)

    out_ref[0] = _ffn_rows(ROW_TILE, x_ref[0], w, s)


def _const_spec(shape):
    zeros = (0,) * len(shape)
    return pl.BlockSpec(shape, lambda b, i: zeros, pipeline_mode=pl.Buffered(1))


def _row_spec():
    return pl.BlockSpec((1, ROW_TILE, D_MODEL), lambda b, i: (b, i, 0))


def _next_row_spec(batch, tiles_per_seq):
    last = batch * tiles_per_seq - 1

    def index(b, i):
        nxt = jnp.minimum(b * tiles_per_seq + i + 1, last)
        return (nxt // tiles_per_seq, nxt % tiles_per_seq, 0)

    return pl.BlockSpec((1, ROW_TILE, D_MODEL), index)


def _meta_spec():
    return pl.BlockSpec((1, N_META, D_MODEL), lambda b, i: (b, 0, 0))


def _compiler_params():
    return pltpu.CompilerParams(dimension_semantics=("arbitrary", "arbitrary"),
                                vmem_limit_bytes=VMEM_LIMIT_BYTES)


def _out_shapes(x, meta):
    return (jax.ShapeDtypeStruct(x.shape, jnp.float32), jax.ShapeDtypeStruct(meta.shape, jnp.float32))


def _mixer_call(layer, x, meta, *weights):
    batch, rows, _ = x.shape
    tm = ROW_TILE
    nc = tm // CHUNK
    f32 = jnp.float32
    return pl.pallas_call(
        functools.partial(_mixer_kernel, layer),
        grid=(batch, rows // tm),
        in_specs=[_row_spec(), _next_row_spec(batch, rows // tm), _meta_spec()]
        + [_const_spec(a.shape) for a in weights],
        out_specs=(_row_spec(), _meta_spec()),
        out_shape=_out_shapes(x, meta),
        scratch_shapes=[
            pltpu.VMEM((8, CONV_WIDTH), f32),
            pltpu.VMEM((16, POOL_WIDTH), f32),
            pltpu.VMEM((HG_HEADS, HG_HEAD_DIM, HG_HEAD_DIM), f32),
            pltpu.VMEM((nc, CHUNK, HG_WIDTH), f32),
            pltpu.VMEM((nc, CHUNK, HG_WIDTH), f32),
            pltpu.VMEM((nc, CHUNK, HG_WIDTH), f32),
            pltpu.VMEM((nc, CHUNK, HG_WIDTH), f32),
            pltpu.VMEM((nc, CHUNK, HG_WIDTH), f32),
            pltpu.VMEM((tm, HG_WIDTH), f32),
            pltpu.VMEM((tm, D_MODEL), jnp.bfloat16),
            pltpu.VMEM((tm, D_MODEL), f32),
            pltpu.VMEM((tm, D_IN), f32),
            pltpu.VMEM((tm, D_MODEL), jnp.bfloat16),
        ],
        compiler_params=_compiler_params(),
        name=f"mixer_l{layer}",
    )(x, x, meta, *weights)


def _ffn_call(layer, x, meta, *weights):
    batch, rows, _ = x.shape
    tm = ROW_TILE
    f32 = jnp.float32
    return pl.pallas_call(
        _ffn_kernel,
        grid=(batch, rows // tm),
        in_specs=[_row_spec(), _meta_spec()] + [_const_spec(a.shape) for a in weights],
        out_specs=(_row_spec(), _meta_spec()),
        out_shape=_out_shapes(x, meta),
        scratch_shapes=[
            pltpu.VMEM((tm, D_MODEL), jnp.bfloat16),
            pltpu.VMEM((2, tm, 2 * FF_CHUNK), f32),
            pltpu.VMEM((N_FF_CHUNKS, 8, 2 * FF_CHUNK), f32),
            pltpu.VMEM((2, tm, FF_GROUP * FF_CHUNK), jnp.bfloat16),
            pltpu.VMEM((tm, D_MODEL), f32),
        ],
        compiler_params=_compiler_params(),
        name=f"ffn_l{layer}",
    )(x, meta, *weights)


def kernel(x, meta_tokens, hg_lower_bounds, w_in, w_conv, w_pool, pool_scale, hg_norm_g, w_o, ln1_g, ln1_b,
           w_up, w_ffn_conv, b_ffn_conv, w_down, ln2_g, ln2_b):
    bf16 = jnp.bfloat16
    batch, seq, _ = x.shape
    assert seq % ROW_TILE == 0 and ROW_TILE % (CHUNK * CHUNK_UNROLL) == 0 and D_IN % PROJ_PIECE == 0
    h = x
    meta = jnp.broadcast_to(meta_tokens[None].astype(x.dtype), (batch, N_META, D_MODEL))
    hb = hg_lower_bounds.astype(jnp.float32)
    for l in range(DEPTH):
        wpool_bd = jax.scipy.linalg.block_diag(*[w_pool[l, g] for g in range(len(POOL_WINDOWS))])
        h, meta = _mixer_call(l, h, meta, hb, w_in[l].astype(bf16), w_conv[l].T, wpool_bd.astype(bf16),
                              pool_scale[l][None], hg_norm_g[l][None], w_o[l].astype(bf16),
                              ln1_g[l][None], ln1_b[l][None])
        h, meta = _ffn_call(l, h, meta, w_up[l].astype(bf16), w_ffn_conv[l].T, b_ffn_conv[l][None],
                            w_down[l].astype(bf16), ln2_g[l][None], ln2_b[l][None])
    return h
```

```python
import functools

import jax
import jax.numpy as jnp
from jax import lax
from jax.experimental import pallas as pl
from jax.experimental.pallas import tpu as pltpu

D_MODEL = 1024
N_META = 16
CONV_WIDTH = 256
HG_WIDTH = 512
HG_HEAD_DIM = 128
HG_HEADS = 4
POOL_WIDTH = 256
POOL_WINDOWS = (2, 4, 8, 16)
POOL_GROUP_DIM = 64
D_FF = 2816
D_IN = 3 * CONV_WIDTH + 4 * HG_WIDTH + POOL_WIDTH
DEPTH = 2
ALPHA = (2 * DEPTH) ** 0.25
LN_EPS = 1e-5
RMS_EPS = 1e-6
F_FLOOR = 1e-30
LOG2_E = 1.4426950408889634

ROW_TILE = 512
FFN_ROW_TILE = 512
CHUNK = 32
HALF = 8
CHUNK_UNROLL = 4
PROJ_PIECE = 256
FF_CHUNK = 256
FF_GROUP = 4
FF_AHEAD = 2
FF_AHEAD_SLOTS = FF_AHEAD + 1
N_FF_CHUNKS = D_FF // FF_CHUNK
VMEM_LIMIT_BYTES = 56 * 1024 * 1024

_C_CONV = 0
_C_HG = 3 * CONV_WIDTH
_C_POOL = _C_HG + 4 * HG_WIDTH


def _layer_norm(r, g, b):
    mu = jnp.mean(r, axis=-1, keepdims=True)
    c = r - mu
    var = jnp.mean(c * c, axis=-1, keepdims=True)
    return c * lax.rsqrt(var + LN_EPS) * g + b


def _sigmoid(x):
    return 1.0 / (1.0 + jnp.exp(-x))


def _shifted_rows(u, prev8, k):
    hist = jnp.concatenate([prev8, u[0:8]], axis=0)
    head = pltpu.roll(hist, k, 0)[8:16]
    if u.shape[0] == 8:
        return head
    return jnp.concatenate([head, pltpu.roll(u, k, 0)[8:]], axis=0)


def _mixer_rows(layer, rows, is_meta, x, proj, between, w, s):
    f32, bf16 = jnp.float32, jnp.bfloat16
    cl = min(CHUNK, rows)
    nc = rows // cl
    tiles = cl // HALF
    between = list(between)

    hh = proj(_C_HG, _C_POOL)
    hq = hh[:, 0:HG_WIDTH]
    fz = hh[:, HG_WIDTH:2 * HG_WIDTH]
    hi = hh[:, 2 * HG_WIDTH:3 * HG_WIDTH]
    gz = hh[:, 3 * HG_WIDTH:4 * HG_WIDTH]

    hb = w.hb[...]
    e = jnp.exp(hb - jnp.max(hb, axis=0, keepdims=True))
    p = e / jnp.sum(e, axis=0, keepdims=True)
    lb = jnp.sum(p[0:layer + 1], axis=0, keepdims=True) - p[0:1]

    sig = _sigmoid(fz)
    f = lb + (1.0 - lb) * sig
    g = (jnp.log(jnp.maximum(f, F_FLOOR)) * LOG2_E).reshape(rows // HALF, HALF, HG_WIDTH)
    rid3 = lax.broadcasted_iota(jnp.int32, g.shape, 1)
    sh = 1
    while sh < HALF:
        g = g + jnp.where(rid3 >= sh, pltpu.roll(g, sh, 1), 0.0)
        sh *= 2
    g = g.reshape(nc, tiles, HALF, HG_WIDTH)
    run = g[:, 0]
    s.g[0:nc, 0:HALF, :] = run
    for j in range(1, tiles):
        run = g[:, j] + run[:, HALF - 1:HALF, :]
        s.g[0:nc, j * HALF:(j + 1) * HALF, :] = run
    s.q[0:nc, 0:cl] = (hq * (HG_HEAD_DIM ** -0.5)).reshape(nc, cl, HG_WIDTH)
    s.k[0:nc, 0:cl] = ((1.0 - lb) * (1.0 - sig)).reshape(nc, cl, HG_WIDTH)
    s.i[0:nc, 0:cl] = hi.reshape(nc, cl, HG_WIDTH)
    s.gate[0:rows] = gz * _sigmoid(gz)

    hc = proj(_C_CONV, _C_HG)
    cb = hc[:, 0:CONV_WIDTH]
    z = hc[:, CONV_WIDTH:2 * CONV_WIDTH] * hc[:, 2 * CONV_WIDTH:3 * CONV_WIDTH]
    zprev = s.zcarry[...]
    s.zcarry[...] = z[rows - 8:rows]
    wc = w.wconv[...]
    conv = wc[0:1] * _shifted_rows(z, zprev, 2) + wc[1:2] * _shifted_rows(z, zprev, 1) + wc[2:3] * z
    s.ymix[0:rows, 0:CONV_WIDTH] = (cb * conv).astype(bf16)

    pv = proj(_C_POOL, D_IN)
    ext = jnp.concatenate([s.vcarry[...], pv], axis=0)
    s.vcarry[...] = pv[rows - 16:rows]
    s2 = ext + pltpu.roll(ext, 1, 0)
    s4 = s2 + pltpu.roll(s2, 2, 0)
    s8 = s4 + pltpu.roll(s4, 4, 0)
    s16 = s8 + pltpu.roll(s8, 8, 0)
    lane = lax.broadcasted_iota(jnp.int32, (1, POOL_WIDTH), 1)
    g0, g1, g2 = lane < POOL_GROUP_DIM, lane < 2 * POOL_GROUP_DIM, lane < 3 * POOL_GROUP_DIM
    win = jnp.where(g0, 2.0, jnp.where(g1, 4.0, jnp.where(g2, 8.0, 16.0))).astype(f32)
    wsum = jnp.where(g0, s2[16:], jnp.where(g1, s4[16:], jnp.where(g2, s8[16:], s16[16:])))
    if is_meta:
        pos = lax.broadcasted_iota(jnp.int32, (rows, 1), 0).astype(f32)
        count = jnp.minimum(pos + 1.0, win)
    else:
        count = win
    dpool = wsum / count - pv
    ypool = jnp.dot(dpool.astype(bf16), w.wpool[...], preferred_element_type=f32) * w.pscale[...]
    s.ymix[0:rows, CONV_WIDTH + HG_WIDTH:D_MODEL] = ypool.astype(bf16)

    s.mix[0:rows] = (
        jnp.dot(s.ymix[0:rows, 0:CONV_WIDTH], w.wo[0:CONV_WIDTH, :], preferred_element_type=f32)
        + jnp.dot(s.ymix[0:rows, CONV_WIDTH + HG_WIDTH:D_MODEL], w.wo[CONV_WIDTH + HG_WIDTH:D_MODEL, :],
                  preferred_element_type=f32))

    rid = lax.broadcasted_iota(jnp.int32, (HALF, 1), 0)
    sgn4 = jnp.where(rid < 4, -1.0, 1.0).astype(f32)
    sgn2 = jnp.where((rid & 2) == 0, -1.0, 1.0).astype(f32)
    odd = (rid & 1) == 1
    lo4 = rid < 4
    nst = HG_HEADS * cl
    tt = lax.broadcasted_iota(jnp.int32, (nst, nst), 0)
    uu = lax.broadcasted_iota(jnp.int32, (nst, nst), 1)
    levels = [m for m in (1, 2, 4, 8, 16, 32, 64) if m < cl]
    masks = [((tt // (2 * m)) == (uu // (2 * m))) & ((tt % (2 * m)) >= m) & ((uu % (2 * m)) < m)
             for m in levels]
    diag = tt == uu
    contract_lanes = (((1,), (1,)), ((), ()))
    head_lanes = [slice(h * HG_HEAD_DIM, (h + 1) * HG_HEAD_DIM) for h in range(HG_HEADS)]

    def stack_heads(v):
        return jnp.concatenate([v[:, ls] for ls in head_lanes], axis=0)

    def level_exponents(c, g_tiles, m):
        out = []
        for j, gt in enumerate(g_tiles):
            r0 = j * HALF
            if m >= HALF:
                ref = (r0 // (2 * m)) * 2 * m + m - 1
                gref = s.g[c, ref:ref + 1, :]
                out.append(gref - gt if (r0 % (2 * m)) < m else gt - gref)
            elif m == 4:
                out.append(sgn4 * (gt - s.g[c, r0 + 3:r0 + 4, :]))
            elif m == 2:
                out.append(sgn2 * (gt - jnp.where(lo4, s.g[c, r0 + 1:r0 + 2, :], s.g[c, r0 + 5:r0 + 6, :])))
            else:
                out.append(jnp.where(odd, gt - pltpu.roll(gt, 1, 0), 0.0))
        return jnp.concatenate(out, axis=0)

    def emit_between():
        if between:
            between.pop(0)()

    def process_chunks(chunks):
        gcs = [s.g[c, 0:cl, :] for c in chunks]
        ibs = [s.i[c, 0:cl, :].astype(bf16) for c in chunks]
        states = [[s.st[h]] for h in range(HG_HEADS)]
        for n, c in enumerate(chunks):
            gl = gcs[n][cl - 1:cl]
            kt = (s.k[c, 0:cl, :] * jnp.exp2(gl - gcs[n])).astype(bf16)
            decay = jnp.exp2(gl)
            for h, ls in enumerate(head_lanes):
                delta = lax.dot_general(ibs[n][:, ls], kt[:, ls], (((0,), (0,)), ((), ())),
                                        preferred_element_type=f32)
                states[h].append(decay[:, ls] * states[h][-1] + delta)
        for h in range(HG_HEADS):
            s.st[h] = states[h][-1]
        emit_between()
        amats = []
        for n, c in enumerate(chunks):
            g_tiles = [gcs[n][j * HALF:(j + 1) * HALF] for j in range(tiles)]
            qb = s.q[c, 0:cl, :].astype(bf16)
            kb = s.k[c, 0:cl, :].astype(bf16)
            a = jnp.where(diag, lax.dot_general(stack_heads(qb), stack_heads(kb), contract_lanes,
                                                preferred_element_type=f32), 0.0)
            for m, mask in zip(levels, masks):
                eb = jnp.exp2(level_exponents(c, g_tiles, m)).astype(bf16)
                a = jnp.where(mask, lax.dot_general(stack_heads(qb * eb), stack_heads(kb * eb), contract_lanes,
                                                    preferred_element_type=f32), a)
            amats.append(a.astype(bf16))
        emit_between()
        inters = []
        for n, c in enumerate(chunks):
            qt = (s.q[c, 0:cl, :] * jnp.exp2(gcs[n])).astype(bf16)
            inters.append([lax.dot_general(qt[:, ls], states[h][n].astype(bf16), contract_lanes,
                                           preferred_element_type=f32) for h, ls in enumerate(head_lanes)])
        emit_between()
        for n, c in enumerate(chunks):
            intra = jnp.dot(amats[n], stack_heads(ibs[n]), preferred_element_type=f32)
            for h, ls in enumerate(head_lanes):
                s.o[c, 0:cl, ls] = inters[n][h] + intra[h * cl:(h + 1) * cl]

    for c0 in range(0, nc, CHUNK_UNROLL):
        process_chunks(list(range(c0, min(c0 + CHUNK_UNROLL, nc))))
    while between:
        emit_between()

    gn = w.gnorm[...]
    for h in range(HG_HEADS):
        ls = slice(h * HG_HEAD_DIM, (h + 1) * HG_HEAD_DIM)
        o = s.o[0:nc, 0:cl, ls].reshape(rows, HG_HEAD_DIM)
        o = o * lax.rsqrt(jnp.mean(o * o, axis=-1, keepdims=True) + RMS_EPS) * gn
        o = o * s.gate[0:rows, ls]
        s.ymix[0:rows, CONV_WIDTH + h * HG_HEAD_DIM:CONV_WIDTH + (h + 1) * HG_HEAD_DIM] = o.astype(bf16)

    mix = s.mix[0:rows] + jnp.dot(s.ymix[0:rows, CONV_WIDTH:CONV_WIDTH + HG_WIDTH],
                                  w.wo[CONV_WIDTH:CONV_WIDTH + HG_WIDTH, :], preferred_element_type=f32)
    return _layer_norm(ALPHA * x + mix, w.lng[...], w.lnb[...])


class _Refs:
    def __init__(self, **kw):
        self.__dict__.update(kw)


def _mixer_kernel(layer, x_ref, xnext_ref, meta_ref, hb, win, wconv, wpool, pscale, gnorm, wo, lng, lnb,
                  out_ref, mout_ref,
                  zcarry, vcarry, st, q, k, i, g, o, gate, ymix, mix, hproj, xnb):
    f32, bf16 = jnp.float32, jnp.bfloat16
    w = _Refs(hb=hb, win=win, wconv=wconv, wpool=wpool, pscale=pscale, gnorm=gnorm, wo=wo, lng=lng, lnb=lnb)
    s = _Refs(zcarry=zcarry, vcarry=vcarry, st=st, q=q, k=k, i=i, g=g, o=o, gate=gate, ymix=ymix, mix=mix)
    first_of_batch = pl.program_id(1) == 0

    @pl.when(first_of_batch & (pl.program_id(0) == 0))
    def _():
        hproj[...] = jnp.dot(x_ref[0].astype(bf16), win[...], preferred_element_type=f32)

    @pl.when(first_of_batch)
    def _():
        zcarry[...] = jnp.zeros(zcarry.shape, f32)
        vcarry[...] = jnp.zeros(vcarry.shape, f32)
        st[...] = jnp.zeros(st.shape, f32)
        mb = meta_ref[0].astype(bf16)
        mout_ref[0] = _mixer_rows(
            layer, N_META, True, meta_ref[0],
            lambda c0, c1: jnp.dot(mb, win[:, c0:c1], preferred_element_type=f32), [], w, s)

    xnb[...] = xnext_ref[0].astype(bf16)

    def piece(c0):
        def run():
            hproj[:, c0:c0 + PROJ_PIECE] = jnp.dot(xnb[...], win[:, c0:c0 + PROJ_PIECE], preferred_element_type=f32)
        return run

    out_ref[0] = _mixer_rows(layer, ROW_TILE, False, x_ref[0], lambda c0, c1: hproj[:, c0:c1],
                             [piece(c0) for c0 in range(0, D_IN, PROJ_PIECE)], w, s)


def _ffn_rows(rows, x, w, s):
    f32, bf16 = jnp.float32, jnp.bfloat16
    s.xb[0:rows] = x.astype(bf16)

    def up_project(j):
        for part in range(2):
            c0 = part * D_FF + j * FF_CHUNK
            s.u[j % FF_AHEAD_SLOTS, 0:rows, part * FF_CHUNK:(part + 1) * FF_CHUNK] = jnp.dot(
                s.xb[0:rows], w.wup[:, c0:c0 + FF_CHUNK], preferred_element_type=f32)

    def conv_rows(j, part):
        c0 = part * D_FF + j * FF_CHUNK
        u = s.u[j % FF_AHEAD_SLOTS, 0:rows, part * FF_CHUNK:(part + 1) * FF_CHUNK]
        prev8 = s.ucarry[j, :, part * FF_CHUNK:(part + 1) * FF_CHUNK]
        s.ucarry[j, :, part * FF_CHUNK:(part + 1) * FF_CHUNK] = u[rows - 8:rows]
        wc = w.wcv[:, c0:c0 + FF_CHUNK]
        return (wc[0:1] * _shifted_rows(u, prev8, 2) + wc[1:2] * _shifted_rows(u, prev8, 1) + wc[2:3] * u
                + w.bcv[:, c0:c0 + FF_CHUNK])

    def down_project(grp):
        j0 = grp * FF_GROUP
        j1 = min(j0 + FF_GROUP, N_FF_CHUNKS)
        return jnp.dot(s.a[grp % 2, 0:rows, 0:(j1 - j0) * FF_CHUNK], w.wdn[j0 * FF_CHUNK:j1 * FF_CHUNK, :],
                       preferred_element_type=f32)

    for j in range(FF_AHEAD):
        up_project(j)
    pending = None
    n_groups = -(-N_FF_CHUNKS // FF_GROUP)
    for j in range(N_FF_CHUNKS):
        if j + FF_AHEAD < N_FF_CHUNKS:
            up_project(j + FF_AHEAD)
        if pending is not None:
            if pending == 0:
                s.acc[0:rows] = down_project(pending)
            else:
                s.acc[0:rows] += down_project(pending)
            pending = None
        gate = conv_rows(j, 0)
        val = conv_rows(j, 1)
        grp, k = divmod(j, FF_GROUP)
        s.a[grp % 2, 0:rows, k * FF_CHUNK:(k + 1) * FF_CHUNK] = (gate * _sigmoid(gate) * val).astype(bf16)
        if k == FF_GROUP - 1 and grp < n_groups - 1:
            pending = grp
    return _layer_norm(ALPHA * x + s.acc[0:rows] + down_project(n_groups - 1), w.lng[...], w.lnb[...])


def _ffn_kernel(x_ref, meta_ref, wup, wcv, bcv, wdn, lng, lnb, out_ref, mout_ref,
                xb, u, ucarry, a, acc):
    w = _Refs(wup=wup, wcv=wcv, bcv=bcv, wdn=wdn, lng=lng, lnb=lnb)
    s = _Refs(xb=xb, u=u, ucarry=ucarry, a=a, acc=acc)

    @pl.when(pl.program_id(1) == 0)
    def _():
        ucarry[...] = jnp.zeros(ucarry.shape, jnp.float32)
        mout_ref[0] = _ffn_rows(N_META, meta_ref[0], w, s)

    out_ref[0] = _ffn_rows(FFN_ROW_TILE, x_ref[0], w, s)


def _weight_spec(a, layer):
    if a.ndim == 3:
        return pl.BlockSpec((None,) + a.shape[1:], lambda b, i: (layer, 0, 0), pipeline_mode=pl.Buffered(1))
    zeros = (0,) * a.ndim
    return pl.BlockSpec(a.shape, lambda b, i: zeros, pipeline_mode=pl.Buffered(1))


def _row_spec(tm):
    return pl.BlockSpec((1, tm, D_MODEL), lambda b, i: (b, i, 0))


def _next_row_spec(batch, tiles_per_seq):
    last = batch * tiles_per_seq - 1

    def index(b, i):
        nxt = jnp.minimum(b * tiles_per_seq + i + 1, last)
        return (nxt // tiles_per_seq, nxt % tiles_per_seq, 0)

    return pl.BlockSpec((1, ROW_TILE, D_MODEL), index)


def _meta_spec():
    return pl.BlockSpec((1, N_META, D_MODEL), lambda b, i: (b, 0, 0))


def _compiler_params():
    return pltpu.CompilerParams(dimension_semantics=("arbitrary", "arbitrary"),
                                vmem_limit_bytes=VMEM_LIMIT_BYTES)


def _out_shapes(x, meta):
    return (jax.ShapeDtypeStruct(x.shape, jnp.float32), jax.ShapeDtypeStruct(meta.shape, jnp.float32))


def _mixer_call(layer, x, meta, *weights):
    batch, rows, _ = x.shape
    tm = ROW_TILE
    nc = tm // CHUNK
    f32 = jnp.float32
    return pl.pallas_call(
        functools.partial(_mixer_kernel, layer),
        grid=(batch, rows // tm),
        in_specs=[_row_spec(tm), _next_row_spec(batch, rows // tm), _meta_spec()]
        + [_weight_spec(a, layer) for a in weights],
        out_specs=(_row_spec(tm), _meta_spec()),
        out_shape=_out_shapes(x, meta),
        scratch_shapes=[
            pltpu.VMEM((8, CONV_WIDTH), f32),
            pltpu.VMEM((16, POOL_WIDTH), f32),
            pltpu.VMEM((HG_HEADS, HG_HEAD_DIM, HG_HEAD_DIM), f32),
            pltpu.VMEM((nc, CHUNK, HG_WIDTH), f32),
            pltpu.VMEM((nc, CHUNK, HG_WIDTH), f32),
            pltpu.VMEM((nc, CHUNK, HG_WIDTH), f32),
            pltpu.VMEM((nc, CHUNK, HG_WIDTH), f32),
            pltpu.VMEM((nc, CHUNK, HG_WIDTH), f32),
            pltpu.VMEM((tm, HG_WIDTH), f32),
            pltpu.VMEM((tm, D_MODEL), jnp.bfloat16),
            pltpu.VMEM((tm, D_MODEL), f32),
            pltpu.VMEM((tm, D_IN), f32),
            pltpu.VMEM((tm, D_MODEL), jnp.bfloat16),
        ],
        compiler_params=_compiler_params(),
        name=f"mixer_l{layer}",
    )(x, x, meta, *weights)


def _ffn_call(layer, x, meta, *weights):
    batch, rows, _ = x.shape
    tm = FFN_ROW_TILE
    f32 = jnp.float32
    return pl.pallas_call(
        _ffn_kernel,
        grid=(batch, rows // tm),
        in_specs=[_row_spec(tm), _meta_spec()] + [_weight_spec(a, layer) for a in weights],
        out_specs=(_row_spec(tm), _meta_spec()),
        out_shape=_out_shapes(x, meta),
        scratch_shapes=[
            pltpu.VMEM((tm, D_MODEL), jnp.bfloat16),
            pltpu.VMEM((FF_AHEAD_SLOTS, tm, 2 * FF_CHUNK), f32),
            pltpu.VMEM((N_FF_CHUNKS, 8, 2 * FF_CHUNK), f32),
            pltpu.VMEM((2, tm, FF_GROUP * FF_CHUNK), jnp.bfloat16),
            pltpu.VMEM((tm, D_MODEL), f32),
        ],
        compiler_params=_compiler_params(),
        name=f"ffn_l{layer}",
    )(x, meta, *weights)


def kernel(x, meta_tokens, hg_lower_bounds, w_in, w_conv, w_pool, pool_scale, hg_norm_g, w_o, ln1_g, ln1_b,
           w_up, w_ffn_conv, b_ffn_conv, w_down, ln2_g, ln2_b):
    bf16 = jnp.bfloat16
    batch, seq, _ = x.shape
    assert seq % ROW_TILE == 0 and ROW_TILE % (CHUNK * CHUNK_UNROLL) == 0 and D_IN % PROJ_PIECE == 0
    assert seq % FFN_ROW_TILE == 0
    h = x
    meta = jnp.broadcast_to(meta_tokens[None].astype(x.dtype), (batch, N_META, D_MODEL))
    hb = hg_lower_bounds.astype(jnp.float32)
    w_in_b, w_o_b, w_up_b, w_down_b = (a.astype(bf16) for a in (w_in, w_o, w_up, w_down))
    for l in range(DEPTH):
        wpool_bd = jax.scipy.linalg.block_diag(*[w_pool[l, g] for g in range(len(POOL_WINDOWS))])
        h, meta = _mixer_call(l, h, meta, hb, w_in_b, w_conv[l].T, wpool_bd.astype(bf16),
                              pool_scale[l][None], hg_norm_g[l][None], w_o_b,
                              ln1_g[l][None], ln1_b[l][None])
        h, meta = _ffn_call(l, h, meta, w_up_b, w_ffn_conv[l].T, b_ffn_conv[l][None],
                            w_down_b, ln2_g[l][None], ln2_b[l][None])
    return h
```

```python
import functools

import jax
import jax.numpy as jnp
from jax import lax
from jax.experimental import pallas as pl
from jax.experimental.pallas import tpu as pltpu

D_MODEL = 1024
N_META = 16
CONV_WIDTH = 256
HG_WIDTH = 512
HG_HEAD_DIM = 128
HG_HEADS = 4
POOL_WIDTH = 256
POOL_WINDOWS = (2, 4, 8, 16)
POOL_GROUP_DIM = 64
D_FF = 2816
D_IN = 3 * CONV_WIDTH + 4 * HG_WIDTH + POOL_WIDTH
DEPTH = 2
ALPHA = (2 * DEPTH) ** 0.25
LN_EPS = 1e-5
RMS_EPS = 1e-6
F_FLOOR = 1e-30
LOG2_E = 1.4426950408889634

ROW_TILE = 512
FFN_ROW_TILE = 512
CHUNK = 32
HALF = 8
CHUNK_UNROLL = 4
PROJ_PIECE = 256
FF_CHUNK = 256
FF_GROUP = 4
FF_AHEAD = 2
FF_AHEAD_SLOTS = FF_AHEAD + 1
N_FF_CHUNKS = D_FF // FF_CHUNK
VMEM_LIMIT_BYTES = 56 * 1024 * 1024

_C_CONV = 0
_C_HG = 3 * CONV_WIDTH
_C_POOL = _C_HG + 4 * HG_WIDTH


def _layer_norm(r, g, b):
    mu = jnp.mean(r, axis=-1, keepdims=True)
    c = r - mu
    var = jnp.mean(c * c, axis=-1, keepdims=True)
    return c * lax.rsqrt(var + LN_EPS) * g + b


def _sigmoid(x):
    return 1.0 / (1.0 + jnp.exp(-x))


def _shifted_rows(u, prev8, k):
    hist = jnp.concatenate([prev8, u[0:8]], axis=0)
    head = pltpu.roll(hist, k, 0)[8:16]
    if u.shape[0] == 8:
        return head
    return jnp.concatenate([head, pltpu.roll(u, k, 0)[8:]], axis=0)


def _mixer_rows(layer, rows, is_meta, x, proj, between, w, s):
    f32, bf16 = jnp.float32, jnp.bfloat16
    cl = min(CHUNK, rows)
    nc = rows // cl
    tiles = cl // HALF
    between = list(between)

    hh = proj(_C_HG, _C_POOL)
    hq = hh[:, 0:HG_WIDTH]
    fz = hh[:, HG_WIDTH:2 * HG_WIDTH]
    hi = hh[:, 2 * HG_WIDTH:3 * HG_WIDTH]
    gz = hh[:, 3 * HG_WIDTH:4 * HG_WIDTH]

    hb = w.hb[...]
    e = jnp.exp(hb - jnp.max(hb, axis=0, keepdims=True))
    p = e / jnp.sum(e, axis=0, keepdims=True)
    lb = jnp.sum(p[0:layer + 1], axis=0, keepdims=True) - p[0:1]

    sig = _sigmoid(fz)
    f = lb + (1.0 - lb) * sig
    g = (jnp.log(jnp.maximum(f, F_FLOOR)) * LOG2_E).reshape(rows // HALF, HALF, HG_WIDTH)
    rid3 = lax.broadcasted_iota(jnp.int32, g.shape, 1)
    sh = 1
    while sh < HALF:
        g = g + jnp.where(rid3 >= sh, pltpu.roll(g, sh, 1), 0.0)
        sh *= 2
    g = g.reshape(nc, tiles, HALF, HG_WIDTH)
    run = g[:, 0]
    s.g[0:nc, 0:HALF, :] = run
    for j in range(1, tiles):
        run = g[:, j] + run[:, HALF - 1:HALF, :]
        s.g[0:nc, j * HALF:(j + 1) * HALF, :] = run
    s.q[0:nc, 0:cl] = (hq * (HG_HEAD_DIM ** -0.5)).reshape(nc, cl, HG_WIDTH)
    s.k[0:nc, 0:cl] = ((1.0 - lb) * (1.0 - sig)).reshape(nc, cl, HG_WIDTH)
    s.i[0:nc, 0:cl] = hi.reshape(nc, cl, HG_WIDTH)
    s.gate[0:rows] = gz * _sigmoid(gz)

    hc = proj(_C_CONV, _C_HG)
    cb = hc[:, 0:CONV_WIDTH]
    z = hc[:, CONV_WIDTH:2 * CONV_WIDTH] * hc[:, 2 * CONV_WIDTH:3 * CONV_WIDTH]
    zprev = s.zcarry[...]
    s.zcarry[...] = z[rows - 8:rows]
    wc = w.wconv[...]
    conv = wc[0:1] * _shifted_rows(z, zprev, 2) + wc[1:2] * _shifted_rows(z, zprev, 1) + wc[2:3] * z
    s.ymix[0:rows, 0:CONV_WIDTH] = (cb * conv).astype(bf16)

    pv = proj(_C_POOL, D_IN)
    ext = jnp.concatenate([s.vcarry[...], pv], axis=0)
    s.vcarry[...] = pv[rows - 16:rows]
    s2 = ext + pltpu.roll(ext, 1, 0)
    s4 = s2 + pltpu.roll(s2, 2, 0)
    s8 = s4 + pltpu.roll(s4, 4, 0)
    s16 = s8 + pltpu.roll(s8, 8, 0)
    lane = lax.broadcasted_iota(jnp.int32, (1, POOL_WIDTH), 1)
    g0, g1, g2 = lane < POOL_GROUP_DIM, lane < 2 * POOL_GROUP_DIM, lane < 3 * POOL_GROUP_DIM
    win = jnp.where(g0, 2.0, jnp.where(g1, 4.0, jnp.where(g2, 8.0, 16.0))).astype(f32)
    wsum = jnp.where(g0, s2[16:], jnp.where(g1, s4[16:], jnp.where(g2, s8[16:], s16[16:])))
    if is_meta:
        pos = lax.broadcasted_iota(jnp.int32, (rows, 1), 0).astype(f32)
        count = jnp.minimum(pos + 1.0, win)
    else:
        count = win
    dpool = wsum / count - pv
    ypool = jnp.dot(dpool.astype(bf16), w.wpool[...], preferred_element_type=f32) * w.pscale[...]
    s.ymix[0:rows, CONV_WIDTH + HG_WIDTH:D_MODEL] = ypool.astype(bf16)

    s.mix[0:rows] = (
        jnp.dot(s.ymix[0:rows, 0:CONV_WIDTH], w.wo[0:CONV_WIDTH, :], preferred_element_type=f32)
        + jnp.dot(s.ymix[0:rows, CONV_WIDTH + HG_WIDTH:D_MODEL], w.wo[CONV_WIDTH + HG_WIDTH:D_MODEL, :],
                  preferred_element_type=f32))

    rid = lax.broadcasted_iota(jnp.int32, (HALF, 1), 0)
    sgn4 = jnp.where(rid < 4, -1.0, 1.0).astype(f32)
    sgn2 = jnp.where((rid & 2) == 0, -1.0, 1.0).astype(f32)
    odd = (rid & 1) == 1
    lo4 = rid < 4
    nst = HG_HEADS * cl
    tt = lax.broadcasted_iota(jnp.int32, (nst, nst), 0)
    uu = lax.broadcasted_iota(jnp.int32, (nst, nst), 1)
    levels = [m for m in (1, 2, 4, 8, 16, 32, 64) if m < cl]
    masks = [((tt // (2 * m)) == (uu // (2 * m))) & ((tt % (2 * m)) >= m) & ((uu % (2 * m)) < m)
             for m in levels]
    diag = tt == uu
    contract_lanes = (((1,), (1,)), ((), ()))
    head_lanes = [slice(h * HG_HEAD_DIM, (h + 1) * HG_HEAD_DIM) for h in range(HG_HEADS)]

    def stack_heads(v):
        return jnp.concatenate([v[:, ls] for ls in head_lanes], axis=0)

    def level_exponents(c, g_tiles, m):
        out = []
        for j, gt in enumerate(g_tiles):
            r0 = j * HALF
            if m >= HALF:
                ref = (r0 // (2 * m)) * 2 * m + m - 1
                gref = s.g[c, ref:ref + 1, :]
                out.append(gref - gt if (r0 % (2 * m)) < m else gt - gref)
            elif m == 4:
                out.append(sgn4 * (gt - s.g[c, r0 + 3:r0 + 4, :]))
            elif m == 2:
                out.append(sgn2 * (gt - jnp.where(lo4, s.g[c, r0 + 1:r0 + 2, :], s.g[c, r0 + 5:r0 + 6, :])))
            else:
                out.append(jnp.where(odd, gt - pltpu.roll(gt, 1, 0), 0.0))
        return jnp.concatenate(out, axis=0)

    def emit_between():
        if between:
            between.pop(0)()

    def process_chunks(chunks):
        gcs = [s.g[c, 0:cl, :] for c in chunks]
        ibs = [s.i[c, 0:cl, :].astype(bf16) for c in chunks]
        states = [[s.st[h]] for h in range(HG_HEADS)]
        for n, c in enumerate(chunks):
            gl = gcs[n][cl - 1:cl]
            kt = (s.k[c, 0:cl, :] * jnp.exp2(gl - gcs[n])).astype(bf16)
            decay = jnp.exp2(gl)
            for h, ls in enumerate(head_lanes):
                delta = lax.dot_general(ibs[n][:, ls], kt[:, ls], (((0,), (0,)), ((), ())),
                                        preferred_element_type=f32)
                states[h].append(decay[:, ls] * states[h][-1] + delta)
        for h in range(HG_HEADS):
            s.st[h] = states[h][-1]
        emit_between()
        amats = []
        for n, c in enumerate(chunks):
            g_tiles = [gcs[n][j * HALF:(j + 1) * HALF] for j in range(tiles)]
            qb = s.q[c, 0:cl, :].astype(bf16)
            kb = s.k[c, 0:cl, :].astype(bf16)
            a = jnp.where(diag, lax.dot_general(stack_heads(qb), stack_heads(kb), contract_lanes,
                                                preferred_element_type=f32), 0.0)
            for m, mask in zip(levels, masks):
                eb = jnp.exp2(level_exponents(c, g_tiles, m)).astype(bf16)
                a = jnp.where(mask, lax.dot_general(stack_heads(qb * eb), stack_heads(kb * eb), contract_lanes,
                                                    preferred_element_type=f32), a)
            amats.append(a.astype(bf16))
        emit_between()
        inters = []
        for n, c in enumerate(chunks):
            qt = (s.q[c, 0:cl, :] * jnp.exp2(gcs[n])).astype(bf16)
            inters.append([jnp.dot(qt[:, ls], states[h][n].T.astype(bf16), preferred_element_type=f32)
                           for h, ls in enumerate(head_lanes)])
        emit_between()
        for n, c in enumerate(chunks):
            intra = jnp.dot(amats[n], stack_heads(ibs[n]), preferred_element_type=f32)
            for h, ls in enumerate(head_lanes):
                s.o[c, 0:cl, ls] = inters[n][h] + intra[h * cl:(h + 1) * cl]

    for c0 in range(0, nc, CHUNK_UNROLL):
        process_chunks(list(range(c0, min(c0 + CHUNK_UNROLL, nc))))
    while between:
        emit_between()

    gn = w.gnorm[...]
    for h in range(HG_HEADS):
        ls = slice(h * HG_HEAD_DIM, (h + 1) * HG_HEAD_DIM)
        o = s.o[0:nc, 0:cl, ls].reshape(rows, HG_HEAD_DIM)
        o = o * lax.rsqrt(jnp.mean(o * o, axis=-1, keepdims=True) + RMS_EPS) * gn
        o = o * s.gate[0:rows, ls]
        s.ymix[0:rows, CONV_WIDTH + h * HG_HEAD_DIM:CONV_WIDTH + (h + 1) * HG_HEAD_DIM] = o.astype(bf16)

    mix = s.mix[0:rows] + jnp.dot(s.ymix[0:rows, CONV_WIDTH:CONV_WIDTH + HG_WIDTH],
                                  w.wo[CONV_WIDTH:CONV_WIDTH + HG_WIDTH, :], preferred_element_type=f32)
    return _layer_norm(ALPHA * x + mix, w.lng[...], w.lnb[...])


class _Refs:
    def __init__(self, **kw):
        self.__dict__.update(kw)


def _mixer_kernel(layer, x_ref, xnext_ref, meta_ref, hb, win, wconv, wpool, pscale, gnorm, wo, lng, lnb,
                  out_ref, mout_ref,
                  zcarry, vcarry, st, q, k, i, g, o, gate, ymix, mix, hproj, xnb):
    f32, bf16 = jnp.float32, jnp.bfloat16
    w = _Refs(hb=hb, win=win, wconv=wconv, wpool=wpool, pscale=pscale, gnorm=gnorm, wo=wo, lng=lng, lnb=lnb)
    s = _Refs(zcarry=zcarry, vcarry=vcarry, st=st, q=q, k=k, i=i, g=g, o=o, gate=gate, ymix=ymix, mix=mix)
    first_of_batch = pl.program_id(1) == 0

    @pl.when(first_of_batch & (pl.program_id(0) == 0))
    def _():
        hproj[...] = jnp.dot(x_ref[0].astype(bf16), win[...], preferred_element_type=f32)

    @pl.when(first_of_batch)
    def _():
        zcarry[...] = jnp.zeros(zcarry.shape, f32)
        vcarry[...] = jnp.zeros(vcarry.shape, f32)
        st[...] = jnp.zeros(st.shape, f32)
        mb = meta_ref[0].astype(bf16)
        mout_ref[0] = _mixer_rows(
            layer, N_META, True, meta_ref[0],
            lambda c0, c1: jnp.dot(mb, win[:, c0:c1], preferred_element_type=f32), [], w, s)

    xnb[...] = xnext_ref[0].astype(bf16)

    def piece(c0):
        def run():
            hproj[:, c0:c0 + PROJ_PIECE] = jnp.dot(xnb[...], win[:, c0:c0 + PROJ_PIECE], preferred_element_type=f32)
        return run

    out_ref[0] = _mixer_rows(layer, ROW_TILE, False, x_ref[0], lambda c0, c1: hproj[:, c0:c1],
                             [piece(c0) for c0 in range(0, D_IN, PROJ_PIECE)], w, s)


def _ffn_rows(rows, x, w, s):
    f32, bf16 = jnp.float32, jnp.bfloat16
    s.xb[0:rows] = x.astype(bf16)

    def up_project(j):
        for part in range(2):
            c0 = part * D_FF + j * FF_CHUNK
            s.u[j % FF_AHEAD_SLOTS, 0:rows, part * FF_CHUNK:(part + 1) * FF_CHUNK] = jnp.dot(
                s.xb[0:rows], w.wup[:, c0:c0 + FF_CHUNK], preferred_element_type=f32)

    def conv_rows(j, part):
        c0 = part * D_FF + j * FF_CHUNK
        u = s.u[j % FF_AHEAD_SLOTS, 0:rows, part * FF_CHUNK:(part + 1) * FF_CHUNK]
        prev8 = s.ucarry[j, :, part * FF_CHUNK:(part + 1) * FF_CHUNK]
        s.ucarry[j, :, part * FF_CHUNK:(part + 1) * FF_CHUNK] = u[rows - 8:rows]
        wc = w.wcv[:, c0:c0 + FF_CHUNK]
        return (wc[0:1] * _shifted_rows(u, prev8, 2) + wc[1:2] * _shifted_rows(u, prev8, 1) + wc[2:3] * u
                + w.bcv[:, c0:c0 + FF_CHUNK])

    def down_project(grp):
        j0 = grp * FF_GROUP
        j1 = min(j0 + FF_GROUP, N_FF_CHUNKS)
        return jnp.dot(s.a[grp % 2, 0:rows, 0:(j1 - j0) * FF_CHUNK], w.wdn[j0 * FF_CHUNK:j1 * FF_CHUNK, :],
                       preferred_element_type=f32)

    for j in range(FF_AHEAD):
        up_project(j)
    pending = None
    n_groups = -(-N_FF_CHUNKS // FF_GROUP)
    for j in range(N_FF_CHUNKS):
        if j + FF_AHEAD < N_FF_CHUNKS:
            up_project(j + FF_AHEAD)
        if pending is not None:
            if pending == 0:
                s.acc[0:rows] = down_project(pending)
            else:
                s.acc[0:rows] += down_project(pending)
            pending = None
        gate = conv_rows(j, 0)
        val = conv_rows(j, 1)
        grp, k = divmod(j, FF_GROUP)
        s.a[grp % 2, 0:rows, k * FF_CHUNK:(k + 1) * FF_CHUNK] = (gate * _sigmoid(gate) * val).astype(bf16)
        if k == FF_GROUP - 1 and grp < n_groups - 1:
            pending = grp
    return _layer_norm(ALPHA * x + s.acc[0:rows] + down_project(n_groups - 1), w.lng[...], w.lnb[...])


def _ffn_kernel(x_ref, meta_ref, wup, wcv, bcv, wdn, lng, lnb, out_ref, mout_ref,
                xb, u, ucarry, a, acc):
    w = _Refs(wup=wup, wcv=wcv, bcv=bcv, wdn=wdn, lng=lng, lnb=lnb)
    s = _Refs(xb=xb, u=u, ucarry=ucarry, a=a, acc=acc)

    @pl.when(pl.program_id(1) == 0)
    def _():
        ucarry[...] = jnp.zeros(ucarry.shape, jnp.float32)
        mout_ref[0] = _ffn_rows(N_META, meta_ref[0], w, s)

    out_ref[0] = _ffn_rows(FFN_ROW_TILE, x_ref[0], w, s)


def _weight_spec(a, layer):
    if a.ndim == 3:
        return pl.BlockSpec((None,) + a.shape[1:], lambda b, i: (layer, 0, 0), pipeline_mode=pl.Buffered(1))
    zeros = (0,) * a.ndim
    return pl.BlockSpec(a.shape, lambda b, i: zeros, pipeline_mode=pl.Buffered(1))


def _row_spec(tm):
    return pl.BlockSpec((1, tm, D_MODEL), lambda b, i: (b, i, 0))


def _next_row_spec(batch, tiles_per_seq):
    last = batch * tiles_per_seq - 1

    def index(b, i):
        nxt = jnp.minimum(b * tiles_per_seq + i + 1, last)
        return (nxt // tiles_per_seq, nxt % tiles_per_seq, 0)

    return pl.BlockSpec((1, ROW_TILE, D_MODEL), index)


def _meta_spec():
    return pl.BlockSpec((1, N_META, D_MODEL), lambda b, i: (b, 0, 0))


def _compiler_params():
    return pltpu.CompilerParams(dimension_semantics=("arbitrary", "arbitrary"),
                                vmem_limit_bytes=VMEM_LIMIT_BYTES)


def _out_shapes(x, meta):
    return (jax.ShapeDtypeStruct(x.shape, jnp.float32), jax.ShapeDtypeStruct(meta.shape, jnp.float32))


def _mixer_call(layer, x, meta, *weights):
    batch, rows, _ = x.shape
    tm = ROW_TILE
    nc = tm // CHUNK
    f32 = jnp.float32
    return pl.pallas_call(
        functools.partial(_mixer_kernel, layer),
        grid=(batch, rows // tm),
        in_specs=[_row_spec(tm), _next_row_spec(batch, rows // tm), _meta_spec()]
        + [_weight_spec(a, layer) for a in weights],
        out_specs=(_row_spec(tm), _meta_spec()),
        out_shape=_out_shapes(x, meta),
        scratch_shapes=[
            pltpu.VMEM((8, CONV_WIDTH), f32),
            pltpu.VMEM((16, POOL_WIDTH), f32),
            pltpu.VMEM((HG_HEADS, HG_HEAD_DIM, HG_HEAD_DIM), f32),
            pltpu.VMEM((nc, CHUNK, HG_WIDTH), f32),
            pltpu.VMEM((nc, CHUNK, HG_WIDTH), f32),
            pltpu.VMEM((nc, CHUNK, HG_WIDTH), f32),
            pltpu.VMEM((nc, CHUNK, HG_WIDTH), f32),
            pltpu.VMEM((nc, CHUNK, HG_WIDTH), f32),
            pltpu.VMEM((tm, HG_WIDTH), f32),
            pltpu.VMEM((tm, D_MODEL), jnp.bfloat16),
            pltpu.VMEM((tm, D_MODEL), f32),
            pltpu.VMEM((tm, D_IN), f32),
            pltpu.VMEM((tm, D_MODEL), jnp.bfloat16),
        ],
        compiler_params=_compiler_params(),
        name=f"mixer_l{layer}",
    )(x, x, meta, *weights)


def _ffn_call(layer, x, meta, *weights):
    batch, rows, _ = x.shape
    tm = FFN_ROW_TILE
    f32 = jnp.float32
    return pl.pallas_call(
        _ffn_kernel,
        grid=(batch, rows // tm),
        in_specs=[_row_spec(tm), _meta_spec()] + [_weight_spec(a, layer) for a in weights],
        out_specs=(_row_spec(tm), _meta_spec()),
        out_shape=_out_shapes(x, meta),
        scratch_shapes=[
            pltpu.VMEM((tm, D_MODEL), jnp.bfloat16),
            pltpu.VMEM((FF_AHEAD_SLOTS, tm, 2 * FF_CHUNK), f32),
            pltpu.VMEM((N_FF_CHUNKS, 8, 2 * FF_CHUNK), f32),
            pltpu.VMEM((2, tm, FF_GROUP * FF_CHUNK), jnp.bfloat16),
            pltpu.VMEM((tm, D_MODEL), f32),
        ],
        compiler_params=_compiler_params(),
        name=f"ffn_l{layer}",
    )(x, meta, *weights)


def kernel(x, meta_tokens, hg_lower_bounds, w_in, w_conv, w_pool, pool_scale, hg_norm_g, w_o, ln1_g, ln1_b,
           w_up, w_ffn_conv, b_ffn_conv, w_down, ln2_g, ln2_b):
    bf16 = jnp.bfloat16
    batch, seq, _ = x.shape
    assert seq % ROW_TILE == 0 and ROW_TILE % (CHUNK * CHUNK_UNROLL) == 0 and D_IN % PROJ_PIECE == 0
    assert seq % FFN_ROW_TILE == 0
    h = x
    meta = jnp.broadcast_to(meta_tokens[None].astype(x.dtype), (batch, N_META, D_MODEL))
    hb = hg_lower_bounds.astype(jnp.float32)
    w_in_b, w_o_b, w_up_b, w_down_b = (a.astype(bf16) for a in (w_in, w_o, w_up, w_down))
    for l in range(DEPTH):
        wpool_bd = jax.scipy.linalg.block_diag(*[w_pool[l, g] for g in range(len(POOL_WINDOWS))])
        h, meta = _mixer_call(l, h, meta, hb, w_in_b, w_conv[l].T, wpool_bd.astype(bf16),
                              pool_scale[l][None], hg_norm_g[l][None], w_o_b,
                              ln1_g[l][None], ln1_b[l][None])
        h, meta = _ffn_call(l, h, meta, w_up_b, w_ffn_conv[l].T, b_ffn_conv[l][None],
                            w_down_b, ln2_g[l][None], ln2_b[l][None])
    return h
```

```python
import functools

import jax
import jax.numpy as jnp
from jax import lax
from jax.experimental import pallas as pl
from jax.experimental.pallas import tpu as pltpu

D_MODEL = 1024
N_META = 16
CONV_WIDTH = 256
HG_WIDTH = 512
HG_HEAD_DIM = 128
HG_HEADS = 4
POOL_WIDTH = 256
POOL_WINDOWS = (2, 4, 8, 16)
POOL_GROUP_DIM = 64
D_FF = 2816
D_IN = 3 * CONV_WIDTH + 4 * HG_WIDTH + POOL_WIDTH
DEPTH = 2
ALPHA = (2 * DEPTH) ** 0.25
LN_EPS = 1e-5
RMS_EPS = 1e-6
F_FLOOR = 1e-30
LOG2_E = 1.4426950408889634

ROW_TILE = 512
FFN_ROW_TILE = 512
CHUNK = 32
HALF = 8
CHUNK_UNROLL = 4
PROJ_PIECE = 256
FF_CHUNK = 256
N_FF_CHUNKS = D_FF // FF_CHUNK
FF_GROUPS = (6, 3, 2)
FF_AHEAD = 2
FF_AHEAD_SLOTS = FF_AHEAD + 1
VMEM_LIMIT_BYTES = 56 * 1024 * 1024

_C_CONV = 0
_C_HG = 3 * CONV_WIDTH
_C_POOL = _C_HG + 4 * HG_WIDTH


def _layer_norm(r, g, b):
    mu = jnp.mean(r, axis=-1, keepdims=True)
    c = r - mu
    var = jnp.mean(c * c, axis=-1, keepdims=True)
    return c * lax.rsqrt(var + LN_EPS) * g + b


def _sigmoid(x):
    return 1.0 / (1.0 + jnp.exp(-x))


def _shifted_rows(u, prev8, k):
    hist = jnp.concatenate([prev8, u[0:8]], axis=0)
    head = pltpu.roll(hist, k, 0)[8:16]
    if u.shape[0] == 8:
        return head
    return jnp.concatenate([head, pltpu.roll(u, k, 0)[8:]], axis=0)


def _mixer_rows(layer, rows, is_meta, x, proj, between, w, s):
    f32, bf16 = jnp.float32, jnp.bfloat16
    cl = min(CHUNK, rows)
    nc = rows // cl
    tiles = cl // HALF
    between = list(between)

    hh = proj(_C_HG, _C_POOL)
    hq = hh[:, 0:HG_WIDTH]
    fz = hh[:, HG_WIDTH:2 * HG_WIDTH]
    hi = hh[:, 2 * HG_WIDTH:3 * HG_WIDTH]
    gz = hh[:, 3 * HG_WIDTH:4 * HG_WIDTH]

    hb = w.hb[...]
    e = jnp.exp(hb - jnp.max(hb, axis=0, keepdims=True))
    p = e / jnp.sum(e, axis=0, keepdims=True)
    lb = jnp.sum(p[0:layer + 1], axis=0, keepdims=True) - p[0:1]

    sig = _sigmoid(fz)
    f = lb + (1.0 - lb) * sig
    g = (jnp.log(jnp.maximum(f, F_FLOOR)) * LOG2_E).reshape(rows // HALF, HALF, HG_WIDTH)
    rid3 = lax.broadcasted_iota(jnp.int32, g.shape, 1)
    sh = 1
    while sh < HALF:
        g = g + jnp.where(rid3 >= sh, pltpu.roll(g, sh, 1), 0.0)
        sh *= 2
    g = g.reshape(nc, tiles, HALF, HG_WIDTH)
    run = g[:, 0]
    s.g[0:nc, 0:HALF, :] = run
    for j in range(1, tiles):
        run = g[:, j] + run[:, HALF - 1:HALF, :]
        s.g[0:nc, j * HALF:(j + 1) * HALF, :] = run
    s.q[0:nc, 0:cl] = (hq * (HG_HEAD_DIM ** -0.5)).reshape(nc, cl, HG_WIDTH)
    s.k[0:nc, 0:cl] = ((1.0 - lb) * (1.0 - sig)).reshape(nc, cl, HG_WIDTH)
    s.i[0:nc, 0:cl] = hi.reshape(nc, cl, HG_WIDTH)
    s.gate[0:rows] = gz * _sigmoid(gz)

    hc = proj(_C_CONV, _C_HG)
    cb = hc[:, 0:CONV_WIDTH]
    z = hc[:, CONV_WIDTH:2 * CONV_WIDTH] * hc[:, 2 * CONV_WIDTH:3 * CONV_WIDTH]
    zprev = s.zcarry[...]
    s.zcarry[...] = z[rows - 8:rows]
    wc = w.wconv[...]
    conv = wc[0:1] * _shifted_rows(z, zprev, 2) + wc[1:2] * _shifted_rows(z, zprev, 1) + wc[2:3] * z
    s.ymix[0:rows, 0:CONV_WIDTH] = (cb * conv).astype(bf16)

    pv = proj(_C_POOL, D_IN)
    ext = jnp.concatenate([s.vcarry[...], pv], axis=0)
    s.vcarry[...] = pv[rows - 16:rows]
    s2 = ext + pltpu.roll(ext, 1, 0)
    s4 = s2 + pltpu.roll(s2, 2, 0)
    s8 = s4 + pltpu.roll(s4, 4, 0)
    s16 = s8 + pltpu.roll(s8, 8, 0)
    lane = lax.broadcasted_iota(jnp.int32, (1, POOL_WIDTH), 1)
    g0, g1, g2 = lane < POOL_GROUP_DIM, lane < 2 * POOL_GROUP_DIM, lane < 3 * POOL_GROUP_DIM
    win = jnp.where(g0, 2.0, jnp.where(g1, 4.0, jnp.where(g2, 8.0, 16.0))).astype(f32)
    wsum = jnp.where(g0, s2[16:], jnp.where(g1, s4[16:], jnp.where(g2, s8[16:], s16[16:])))
    if is_meta:
        pos = lax.broadcasted_iota(jnp.int32, (rows, 1), 0).astype(f32)
        count = jnp.minimum(pos + 1.0, win)
    else:
        count = win
    dpool = wsum / count - pv
    ypool = jnp.dot(dpool.astype(bf16), w.wpool[...], preferred_element_type=f32) * w.pscale[...]
    s.ymix[0:rows, CONV_WIDTH + HG_WIDTH:D_MODEL] = ypool.astype(bf16)

    s.mix[0:rows] = (
        jnp.dot(s.ymix[0:rows, 0:CONV_WIDTH], w.wo[0:CONV_WIDTH, :], preferred_element_type=f32)
        + jnp.dot(s.ymix[0:rows, CONV_WIDTH + HG_WIDTH:D_MODEL], w.wo[CONV_WIDTH + HG_WIDTH:D_MODEL, :],
                  preferred_element_type=f32))

    rid = lax.broadcasted_iota(jnp.int32, (HALF, 1), 0)
    sgn4 = jnp.where(rid < 4, -1.0, 1.0).astype(f32)
    sgn2 = jnp.where((rid & 2) == 0, -1.0, 1.0).astype(f32)
    odd = (rid & 1) == 1
    lo4 = rid < 4
    nst = HG_HEADS * cl
    tt = lax.broadcasted_iota(jnp.int32, (nst, nst), 0)
    uu = lax.broadcasted_iota(jnp.int32, (nst, nst), 1)
    levels = [m for m in (1, 2, 4, 8, 16, 32, 64) if m < cl]
    masks = [((tt // (2 * m)) == (uu // (2 * m))) & ((tt % (2 * m)) >= m) & ((uu % (2 * m)) < m)
             for m in levels]
    diag = tt == uu
    contract_lanes = (((1,), (1,)), ((), ()))
    head_lanes = [slice(h * HG_HEAD_DIM, (h + 1) * HG_HEAD_DIM) for h in range(HG_HEADS)]

    def stack_heads(v):
        return jnp.concatenate([v[:, ls] for ls in head_lanes], axis=0)

    def level_exponents(c, g_tiles, m):
        out = []
        for j, gt in enumerate(g_tiles):
            r0 = j * HALF
            if m >= HALF:
                ref = (r0 // (2 * m)) * 2 * m + m - 1
                gref = s.g[c, ref:ref + 1, :]
                out.append(gref - gt if (r0 % (2 * m)) < m else gt - gref)
            elif m == 4:
                out.append(sgn4 * (gt - s.g[c, r0 + 3:r0 + 4, :]))
            elif m == 2:
                out.append(sgn2 * (gt - jnp.where(lo4, s.g[c, r0 + 1:r0 + 2, :], s.g[c, r0 + 5:r0 + 6, :])))
            else:
                out.append(jnp.where(odd, gt - pltpu.roll(gt, 1, 0), 0.0))
        return jnp.concatenate(out, axis=0)

    def emit_between():
        if between:
            between.pop(0)()

    def process_chunks(chunks):
        gcs = [s.g[c, 0:cl, :] for c in chunks]
        ibs = [s.i[c, 0:cl, :].astype(bf16) for c in chunks]
        states = [[s.st[h]] for h in range(HG_HEADS)]
        for n, c in enumerate(chunks):
            gl = gcs[n][cl - 1:cl]
            kt = (s.k[c, 0:cl, :] * jnp.exp2(gl - gcs[n])).astype(bf16)
            decay = jnp.exp2(gl)
            for h, ls in enumerate(head_lanes):
                delta = lax.dot_general(ibs[n][:, ls], kt[:, ls], (((0,), (0,)), ((), ())),
                                        preferred_element_type=f32)
                states[h].append(decay[:, ls] * states[h][-1] + delta)
        for h in range(HG_HEADS):
            s.st[h] = states[h][-1]
        emit_between()
        amats = []
        for n, c in enumerate(chunks):
            g_tiles = [gcs[n][j * HALF:(j + 1) * HALF] for j in range(tiles)]
            qb = s.q[c, 0:cl, :].astype(bf16)
            kb = s.k[c, 0:cl, :].astype(bf16)
            a = jnp.where(diag, lax.dot_general(stack_heads(qb), stack_heads(kb), contract_lanes,
                                                preferred_element_type=f32), 0.0)
            for m, mask in zip(levels, masks):
                eb = jnp.exp2(level_exponents(c, g_tiles, m)).astype(bf16)
                a = jnp.where(mask, lax.dot_general(stack_heads(qb * eb), stack_heads(kb * eb), contract_lanes,
                                                    preferred_element_type=f32), a)
            amats.append(a.astype(bf16))
        emit_between()
        inters = []
        for n, c in enumerate(chunks):
            qt = (s.q[c, 0:cl, :] * jnp.exp2(gcs[n])).astype(bf16)
            inters.append([jnp.dot(qt[:, ls], states[h][n].T.astype(bf16), preferred_element_type=f32)
                           for h, ls in enumerate(head_lanes)])
        emit_between()
        for n, c in enumerate(chunks):
            intra = jnp.dot(amats[n], stack_heads(ibs[n]), preferred_element_type=f32)
            for h, ls in enumerate(head_lanes):
                s.o[c, 0:cl, ls] = inters[n][h] + intra[h * cl:(h + 1) * cl]

    for c0 in range(0, nc, CHUNK_UNROLL):
        process_chunks(list(range(c0, min(c0 + CHUNK_UNROLL, nc))))
    while between:
        emit_between()

    gn = w.gnorm[...]
    for h in range(HG_HEADS):
        ls = slice(h * HG_HEAD_DIM, (h + 1) * HG_HEAD_DIM)
        o = s.o[0:nc, 0:cl, ls].reshape(rows, HG_HEAD_DIM)
        o = o * lax.rsqrt(jnp.mean(o * o, axis=-1, keepdims=True) + RMS_EPS) * gn
        o = o * s.gate[0:rows, ls]
        s.ymix[0:rows, CONV_WIDTH + h * HG_HEAD_DIM:CONV_WIDTH + (h + 1) * HG_HEAD_DIM] = o.astype(bf16)

    mix = s.mix[0:rows] + jnp.dot(s.ymix[0:rows, CONV_WIDTH:CONV_WIDTH + HG_WIDTH],
                                  w.wo[CONV_WIDTH:CONV_WIDTH + HG_WIDTH, :], preferred_element_type=f32)
    return _layer_norm(ALPHA * x + mix, w.lng[...], w.lnb[...])


class _Refs:
    def __init__(self, **kw):
        self.__dict__.update(kw)


def _mixer_kernel(layer, x_ref, xnext_ref, meta_ref, hb, win, wconv, wpool, pscale, gnorm, wo, lng, lnb,
                  out_ref, mout_ref,
                  zcarry, vcarry, st, q, k, i, g, o, gate, ymix, mix, hproj, xnb):
    f32, bf16 = jnp.float32, jnp.bfloat16
    w = _Refs(hb=hb, win=win, wconv=wconv, wpool=wpool, pscale=pscale, gnorm=gnorm, wo=wo, lng=lng, lnb=lnb)
    s = _Refs(zcarry=zcarry, vcarry=vcarry, st=st, q=q, k=k, i=i, g=g, o=o, gate=gate, ymix=ymix, mix=mix)
    first_of_batch = pl.program_id(1) == 0

    @pl.when(first_of_batch & (pl.program_id(0) == 0))
    def _():
        hproj[...] = jnp.dot(x_ref[0].astype(bf16), win[...], preferred_element_type=f32)

    @pl.when(first_of_batch)
    def _():
        zcarry[...] = jnp.zeros(zcarry.shape, f32)
        vcarry[...] = jnp.zeros(vcarry.shape, f32)
        st[...] = jnp.zeros(st.shape, f32)
        mb = meta_ref[0].astype(bf16)
        mout_ref[0] = _mixer_rows(
            layer, N_META, True, meta_ref[0],
            lambda c0, c1: jnp.dot(mb, win[:, c0:c1], preferred_element_type=f32), [], w, s)

    xnb[...] = xnext_ref[0].astype(bf16)

    def piece(c0):
        def run():
            hproj[:, c0:c0 + PROJ_PIECE] = jnp.dot(xnb[...], win[:, c0:c0 + PROJ_PIECE], preferred_element_type=f32)
        return run

    out_ref[0] = _mixer_rows(layer, ROW_TILE, False, x_ref[0], lambda c0, c1: hproj[:, c0:c1],
                             [piece(c0) for c0 in range(0, D_IN, PROJ_PIECE)], w, s)


def _ffn_rows(rows, x, w, s):
    f32, bf16 = jnp.float32, jnp.bfloat16
    s.xb[0:rows] = x.astype(bf16)

    def up_project(j):
        for part in range(2):
            c0 = part * D_FF + j * FF_CHUNK
            s.u[j % FF_AHEAD_SLOTS, 0:rows, part * FF_CHUNK:(part + 1) * FF_CHUNK] = jnp.dot(
                s.xb[0:rows], w.wup[:, c0:c0 + FF_CHUNK], preferred_element_type=f32)

    def conv_rows(j, part):
        c0 = part * D_FF + j * FF_CHUNK
        u = s.u[j % FF_AHEAD_SLOTS, 0:rows, part * FF_CHUNK:(part + 1) * FF_CHUNK]
        prev8 = s.ucarry[j, :, part * FF_CHUNK:(part + 1) * FF_CHUNK]
        s.ucarry[j, :, part * FF_CHUNK:(part + 1) * FF_CHUNK] = u[rows - 8:rows]
        wc = w.wcv[:, c0:c0 + FF_CHUNK]
        return (wc[0:1] * _shifted_rows(u, prev8, 2) + wc[1:2] * _shifted_rows(u, prev8, 1) + wc[2:3] * u
                + w.bcv[:, c0:c0 + FF_CHUNK])

    starts = [sum(FF_GROUPS[:g]) for g in range(len(FF_GROUPS) + 1)]
    group_of = {j: g for g in range(len(FF_GROUPS)) for j in range(starts[g], starts[g + 1])}

    def down_project(grp):
        j0, j1 = starts[grp], starts[grp + 1]
        return jnp.dot(s.a[grp % 2, 0:rows, 0:(j1 - j0) * FF_CHUNK], w.wdn[j0 * FF_CHUNK:j1 * FF_CHUNK, :],
                       preferred_element_type=f32)

    for j in range(FF_AHEAD):
        up_project(j)
    pending = None
    for j in range(N_FF_CHUNKS):
        if j + FF_AHEAD < N_FF_CHUNKS:
            up_project(j + FF_AHEAD)
        if pending is not None:
            if pending == 0:
                s.acc[0:rows] = down_project(pending)
            else:
                s.acc[0:rows] += down_project(pending)
            pending = None
        gate = conv_rows(j, 0)
        val = conv_rows(j, 1)
        grp = group_of[j]
        k = j - starts[grp]
        s.a[grp % 2, 0:rows, k * FF_CHUNK:(k + 1) * FF_CHUNK] = (gate * _sigmoid(gate) * val).astype(bf16)
        if j + 1 == starts[grp + 1] and grp < len(FF_GROUPS) - 1:
            pending = grp
    return _layer_norm(ALPHA * x + s.acc[0:rows] + down_project(len(FF_GROUPS) - 1), w.lng[...], w.lnb[...])


def _ffn_kernel(x_ref, meta_ref, wup, wcv, bcv, wdn, lng, lnb, out_ref, mout_ref,
                xb, u, ucarry, a, acc):
    w = _Refs(wup=wup, wcv=wcv, bcv=bcv, wdn=wdn, lng=lng, lnb=lnb)
    s = _Refs(xb=xb, u=u, ucarry=ucarry, a=a, acc=acc)

    @pl.when(pl.program_id(1) == 0)
    def _():
        ucarry[...] = jnp.zeros(ucarry.shape, jnp.float32)
        mout_ref[0] = _ffn_rows(N_META, meta_ref[0], w, s)

    out_ref[0] = _ffn_rows(FFN_ROW_TILE, x_ref[0], w, s)


def _weight_spec(a, layer):
    if a.ndim == 3:
        return pl.BlockSpec((None,) + a.shape[1:], lambda b, i: (layer, 0, 0), pipeline_mode=pl.Buffered(1))
    zeros = (0,) * a.ndim
    return pl.BlockSpec(a.shape, lambda b, i: zeros, pipeline_mode=pl.Buffered(1))


def _row_spec(tm):
    return pl.BlockSpec((1, tm, D_MODEL), lambda b, i: (b, i, 0))


def _next_row_spec(batch, tiles_per_seq):
    last = batch * tiles_per_seq - 1

    def index(b, i):
        nxt = jnp.minimum(b * tiles_per_seq + i + 1, last)
        return (nxt // tiles_per_seq, nxt % tiles_per_seq, 0)

    return pl.BlockSpec((1, ROW_TILE, D_MODEL), index)


def _meta_spec():
    return pl.BlockSpec((1, N_META, D_MODEL), lambda b, i: (b, 0, 0))


def _compiler_params():
    return pltpu.CompilerParams(dimension_semantics=("arbitrary", "arbitrary"),
                                vmem_limit_bytes=VMEM_LIMIT_BYTES)


def _out_shapes(x, meta):
    return (jax.ShapeDtypeStruct(x.shape, jnp.float32), jax.ShapeDtypeStruct(meta.shape, jnp.float32))


def _mixer_call(layer, x, meta, *weights):
    batch, rows, _ = x.shape
    tm = ROW_TILE
    nc = tm // CHUNK
    f32 = jnp.float32
    return pl.pallas_call(
        functools.partial(_mixer_kernel, layer),
        grid=(batch, rows // tm),
        in_specs=[_row_spec(tm), _next_row_spec(batch, rows // tm), _meta_spec()]
        + [_weight_spec(a, layer) for a in weights],
        out_specs=(_row_spec(tm), _meta_spec()),
        out_shape=_out_shapes(x, meta),
        scratch_shapes=[
            pltpu.VMEM((8, CONV_WIDTH), f32),
            pltpu.VMEM((16, POOL_WIDTH), f32),
            pltpu.VMEM((HG_HEADS, HG_HEAD_DIM, HG_HEAD_DIM), f32),
            pltpu.VMEM((nc, CHUNK, HG_WIDTH), f32),
            pltpu.VMEM((nc, CHUNK, HG_WIDTH), f32),
            pltpu.VMEM((nc, CHUNK, HG_WIDTH), f32),
            pltpu.VMEM((nc, CHUNK, HG_WIDTH), f32),
            pltpu.VMEM((nc, CHUNK, HG_WIDTH), f32),
            pltpu.VMEM((tm, HG_WIDTH), f32),
            pltpu.VMEM((tm, D_MODEL), jnp.bfloat16),
            pltpu.VMEM((tm, D_MODEL), f32),
            pltpu.VMEM((tm, D_IN), f32),
            pltpu.VMEM((tm, D_MODEL), jnp.bfloat16),
        ],
        compiler_params=_compiler_params(),
        name=f"mixer_l{layer}",
    )(x, x, meta, *weights)


def _ffn_call(layer, x, meta, *weights):
    batch, rows, _ = x.shape
    tm = FFN_ROW_TILE
    f32 = jnp.float32
    return pl.pallas_call(
        _ffn_kernel,
        grid=(batch, rows // tm),
        in_specs=[_row_spec(tm), _meta_spec()] + [_weight_spec(a, layer) for a in weights],
        out_specs=(_row_spec(tm), _meta_spec()),
        out_shape=_out_shapes(x, meta),
        scratch_shapes=[
            pltpu.VMEM((tm, D_MODEL), jnp.bfloat16),
            pltpu.VMEM((FF_AHEAD_SLOTS, tm, 2 * FF_CHUNK), f32),
            pltpu.VMEM((N_FF_CHUNKS, 8, 2 * FF_CHUNK), f32),
            pltpu.VMEM((2, tm, max(FF_GROUPS) * FF_CHUNK), jnp.bfloat16),
            pltpu.VMEM((tm, D_MODEL), f32),
        ],
        compiler_params=_compiler_params(),
        name=f"ffn_l{layer}",
    )(x, meta, *weights)


def kernel(x, meta_tokens, hg_lower_bounds, w_in, w_conv, w_pool, pool_scale, hg_norm_g, w_o, ln1_g, ln1_b,
           w_up, w_ffn_conv, b_ffn_conv, w_down, ln2_g, ln2_b):
    bf16 = jnp.bfloat16
    batch, seq, _ = x.shape
    assert seq % ROW_TILE == 0 and ROW_TILE % (CHUNK * CHUNK_UNROLL) == 0 and D_IN % PROJ_PIECE == 0
    assert seq % FFN_ROW_TILE == 0 and sum(FF_GROUPS) == N_FF_CHUNKS
    h = x
    meta = jnp.broadcast_to(meta_tokens[None].astype(x.dtype), (batch, N_META, D_MODEL))
    hb = hg_lower_bounds.astype(jnp.float32)
    w_in_b, w_o_b, w_up_b, w_down_b = (a.astype(bf16) for a in (w_in, w_o, w_up, w_down))
    for l in range(DEPTH):
        wpool_bd = jax.scipy.linalg.block_diag(*[w_pool[l, g] for g in range(len(POOL_WINDOWS))])
        h, meta = _mixer_call(l, h, meta, hb, w_in_b, w_conv[l].T, wpool_bd.astype(bf16),
                              pool_scale[l][None], hg_norm_g[l][None], w_o_b,
                              ln1_g[l][None], ln1_b[l][None])
        h, meta = _ffn_call(l, h, meta, w_up_b, w_ffn_conv[l].T, b_ffn_conv[l][None],
                            w_down_b, ln2_g[l][None], ln2_b[l][None])
    return h
```

```python
import functools

import jax
import jax.numpy as jnp
from jax import lax
from jax.experimental import pallas as pl
from jax.experimental.pallas import tpu as pltpu

D_MODEL = 1024
N_META = 16
CONV_WIDTH = 256
HG_WIDTH = 512
HG_HEAD_DIM = 128
HG_HEADS = 4
POOL_WIDTH = 256
POOL_WINDOWS = (2, 4, 8, 16)
POOL_GROUP_DIM = 64
D_FF = 2816
D_IN = 3 * CONV_WIDTH + 4 * HG_WIDTH + POOL_WIDTH
DEPTH = 2
ALPHA = (2 * DEPTH) ** 0.25
LN_EPS = 1e-5
RMS_EPS = 1e-6
F_FLOOR = 1e-30
LOG2_E = 1.4426950408889634

ROW_TILE = 512
FFN_ROW_TILE = 512
CHUNK = 32
HALF = 8
CHUNK_UNROLL = 4
PROJ_PIECE = 256
FF_CHUNK = 256
N_FF_CHUNKS = D_FF // FF_CHUNK
FF_GROUPS = (6, 3, 2)
FF_AHEAD = 2
FF_AHEAD_SLOTS = FF_AHEAD + 1
VMEM_LIMIT_BYTES = 56 * 1024 * 1024

_C_CONV = 0
_C_HG = 3 * CONV_WIDTH
_C_POOL = _C_HG + 4 * HG_WIDTH


def _layer_norm(r, g, b):
    mu = jnp.mean(r, axis=-1, keepdims=True)
    c = r - mu
    var = jnp.mean(c * c, axis=-1, keepdims=True)
    return c * lax.rsqrt(var + LN_EPS) * g + b


def _sigmoid(x):
    return 1.0 / (1.0 + jnp.exp(-x))


def _shifted_rows(u, prev8, k):
    hist = jnp.concatenate([prev8, u[0:8]], axis=0)
    head = pltpu.roll(hist, k, 0)[8:16]
    if u.shape[0] == 8:
        return head
    return jnp.concatenate([head, pltpu.roll(u, k, 0)[8:]], axis=0)


def _mixer_rows(layer, rows, is_meta, x, proj, between, w, s):
    f32, bf16 = jnp.float32, jnp.bfloat16
    cl = min(CHUNK, rows)
    nc = rows // cl
    tiles = cl // HALF
    between = list(between)

    hh = proj(_C_HG, _C_POOL)
    hq = hh[:, 0:HG_WIDTH]
    fz = hh[:, HG_WIDTH:2 * HG_WIDTH]
    hi = hh[:, 2 * HG_WIDTH:3 * HG_WIDTH]
    gz = hh[:, 3 * HG_WIDTH:4 * HG_WIDTH]

    hb = w.hb[...]
    e = jnp.exp(hb - jnp.max(hb, axis=0, keepdims=True))
    p = e / jnp.sum(e, axis=0, keepdims=True)
    lb = jnp.sum(p[0:layer + 1], axis=0, keepdims=True) - p[0:1]

    sig = _sigmoid(fz)
    f = lb + (1.0 - lb) * sig
    g = (jnp.log(jnp.maximum(f, F_FLOOR)) * LOG2_E).reshape(rows // HALF, HALF, HG_WIDTH)
    rid3 = lax.broadcasted_iota(jnp.int32, g.shape, 1)
    sh = 1
    while sh < HALF:
        g = g + jnp.where(rid3 >= sh, pltpu.roll(g, sh, 1), 0.0)
        sh *= 2
    g = g.reshape(nc, tiles, HALF, HG_WIDTH)
    run = g[:, 0]
    s.g[0:nc, 0:HALF, :] = run
    for j in range(1, tiles):
        run = g[:, j] + run[:, HALF - 1:HALF, :]
        s.g[0:nc, j * HALF:(j + 1) * HALF, :] = run
    s.q[0:nc, 0:cl] = (hq * (HG_HEAD_DIM ** -0.5)).reshape(nc, cl, HG_WIDTH)
    s.k[0:nc, 0:cl] = ((1.0 - lb) * (1.0 - sig)).reshape(nc, cl, HG_WIDTH)
    s.i[0:nc, 0:cl] = hi.reshape(nc, cl, HG_WIDTH)
    s.gate[0:rows] = gz * _sigmoid(gz)

    hc = proj(_C_CONV, _C_HG)
    cb = hc[:, 0:CONV_WIDTH]
    z = hc[:, CONV_WIDTH:2 * CONV_WIDTH] * hc[:, 2 * CONV_WIDTH:3 * CONV_WIDTH]
    zprev = s.zcarry[0:8]
    s.zcarry[0:8] = z[rows - 8:rows]
    wc = w.wconv[...]
    conv = wc[0:1] * _shifted_rows(z, zprev, 2) + wc[1:2] * _shifted_rows(z, zprev, 1) + wc[2:3] * z
    s.ymix[0:rows, 0:CONV_WIDTH] = (cb * conv).astype(bf16)

    pv = proj(_C_POOL, D_IN)
    ext = jnp.concatenate([s.vcarry[...], pv], axis=0)
    s.vcarry[...] = pv[rows - 16:rows]
    s2 = ext + pltpu.roll(ext, 1, 0)
    s4 = s2 + pltpu.roll(s2, 2, 0)
    s8 = s4 + pltpu.roll(s4, 4, 0)
    s16 = s8 + pltpu.roll(s8, 8, 0)
    lane = lax.broadcasted_iota(jnp.int32, (1, POOL_WIDTH), 1)
    g0, g1, g2 = lane < POOL_GROUP_DIM, lane < 2 * POOL_GROUP_DIM, lane < 3 * POOL_GROUP_DIM
    win = jnp.where(g0, 2.0, jnp.where(g1, 4.0, jnp.where(g2, 8.0, 16.0))).astype(f32)
    wsum = jnp.where(g0, s2[16:], jnp.where(g1, s4[16:], jnp.where(g2, s8[16:], s16[16:])))
    if is_meta:
        pos = lax.broadcasted_iota(jnp.int32, (rows, 1), 0).astype(f32)
        count = jnp.minimum(pos + 1.0, win)
    else:
        count = win
    dpool = wsum / count - pv
    ypool = jnp.dot(dpool.astype(bf16), w.wpool[...], preferred_element_type=f32) * w.pscale[...]
    s.ymix[0:rows, CONV_WIDTH + HG_WIDTH:D_MODEL] = ypool.astype(bf16)

    s.mix[0:rows] = (
        jnp.dot(s.ymix[0:rows, 0:CONV_WIDTH], w.wo[0:CONV_WIDTH, :], preferred_element_type=f32)
        + jnp.dot(s.ymix[0:rows, CONV_WIDTH + HG_WIDTH:D_MODEL], w.wo[CONV_WIDTH + HG_WIDTH:D_MODEL, :],
                  preferred_element_type=f32))

    rid = lax.broadcasted_iota(jnp.int32, (HALF, 1), 0)
    sgn4 = jnp.where(rid < 4, -1.0, 1.0).astype(f32)
    sgn2 = jnp.where((rid & 2) == 0, -1.0, 1.0).astype(f32)
    odd = (rid & 1) == 1
    lo4 = rid < 4
    nst = HG_HEADS * cl
    tt = lax.broadcasted_iota(jnp.int32, (nst, nst), 0)
    uu = lax.broadcasted_iota(jnp.int32, (nst, nst), 1)
    levels = [m for m in (1, 2, 4, 8, 16, 32, 64) if m < cl]
    masks = [((tt // (2 * m)) == (uu // (2 * m))) & ((tt % (2 * m)) >= m) & ((uu % (2 * m)) < m)
             for m in levels]
    diag = tt == uu
    contract_lanes = (((1,), (1,)), ((), ()))
    head_lanes = [slice(h * HG_HEAD_DIM, (h + 1) * HG_HEAD_DIM) for h in range(HG_HEADS)]

    def stack_heads(v):
        return jnp.concatenate([v[:, ls] for ls in head_lanes], axis=0)

    def level_exponents(c, g_tiles, m):
        out = []
        for j, gt in enumerate(g_tiles):
            r0 = j * HALF
            if m >= HALF:
                ref = (r0 // (2 * m)) * 2 * m + m - 1
                gref = s.g[c, ref:ref + 1, :]
                out.append(gref - gt if (r0 % (2 * m)) < m else gt - gref)
            elif m == 4:
                out.append(sgn4 * (gt - s.g[c, r0 + 3:r0 + 4, :]))
            elif m == 2:
                out.append(sgn2 * (gt - jnp.where(lo4, s.g[c, r0 + 1:r0 + 2, :], s.g[c, r0 + 5:r0 + 6, :])))
            else:
                out.append(jnp.where(odd, gt - pltpu.roll(gt, 1, 0), 0.0))
        return jnp.concatenate(out, axis=0)

    def emit_between():
        if between:
            between.pop(0)()

    def process_chunks(chunks):
        gcs = [s.g[c, 0:cl, :] for c in chunks]
        ibs = [s.i[c, 0:cl, :].astype(bf16) for c in chunks]
        states = [[s.st[h]] for h in range(HG_HEADS)]
        for n, c in enumerate(chunks):
            gl = gcs[n][cl - 1:cl]
            kt = (s.k[c, 0:cl, :] * jnp.exp2(gl - gcs[n])).astype(bf16)
            decay = jnp.exp2(gl)
            for h, ls in enumerate(head_lanes):
                delta = lax.dot_general(ibs[n][:, ls], kt[:, ls], (((0,), (0,)), ((), ())),
                                        preferred_element_type=f32)
                states[h].append(decay[:, ls] * states[h][-1] + delta)
        for h in range(HG_HEADS):
            s.st[h] = states[h][-1]
        emit_between()
        amats = []
        for n, c in enumerate(chunks):
            g_tiles = [gcs[n][j * HALF:(j + 1) * HALF] for j in range(tiles)]
            qb = s.q[c, 0:cl, :].astype(bf16)
            kb = s.k[c, 0:cl, :].astype(bf16)
            a = jnp.where(diag, lax.dot_general(stack_heads(qb), stack_heads(kb), contract_lanes,
                                                preferred_element_type=f32), 0.0)
            for m, mask in zip(levels, masks):
                eb = jnp.exp2(level_exponents(c, g_tiles, m)).astype(bf16)
                a = jnp.where(mask, lax.dot_general(stack_heads(qb * eb), stack_heads(kb * eb), contract_lanes,
                                                    preferred_element_type=f32), a)
            amats.append(a.astype(bf16))
        emit_between()
        inters = []
        for n, c in enumerate(chunks):
            qt = (s.q[c, 0:cl, :] * jnp.exp2(gcs[n])).astype(bf16)
            inters.append([jnp.dot(qt[:, ls], states[h][n].T.astype(bf16), preferred_element_type=f32)
                           for h, ls in enumerate(head_lanes)])
        emit_between()
        for n, c in enumerate(chunks):
            intra = jnp.dot(amats[n], stack_heads(ibs[n]), preferred_element_type=f32)
            for h, ls in enumerate(head_lanes):
                s.o[c, 0:cl, ls] = inters[n][h] + intra[h * cl:(h + 1) * cl]

    for c0 in range(0, nc, CHUNK_UNROLL):
        process_chunks(list(range(c0, min(c0 + CHUNK_UNROLL, nc))))
    while between:
        emit_between()

    gn = w.gnorm[...]
    for h in range(HG_HEADS):
        ls = slice(h * HG_HEAD_DIM, (h + 1) * HG_HEAD_DIM)
        o = s.o[0:nc, 0:cl, ls].reshape(rows, HG_HEAD_DIM)
        o = o * lax.rsqrt(jnp.mean(o * o, axis=-1, keepdims=True) + RMS_EPS) * gn
        o = o * s.gate[0:rows, ls]
        s.ymix[0:rows, CONV_WIDTH + h * HG_HEAD_DIM:CONV_WIDTH + (h + 1) * HG_HEAD_DIM] = o.astype(bf16)

    mix = s.mix[0:rows] + jnp.dot(s.ymix[0:rows, CONV_WIDTH:CONV_WIDTH + HG_WIDTH],
                                  w.wo[CONV_WIDTH:CONV_WIDTH + HG_WIDTH, :], preferred_element_type=f32)
    return _layer_norm(ALPHA * x + mix, w.lng[...], w.lnb[...])


class _Refs:
    def __init__(self, **kw):
        self.__dict__.update(kw)


def _mixer_kernel(layer, x_ref, xnext_ref, meta_ref, win, wpool, wo, hb, wconv, pscale, gnorm, lng, lnb,
                  out_ref, mout_ref,
                  zcarry, vcarry, st, q, k, i, g, o, gate, ymix, mix, hproj, xnb):
    f32, bf16 = jnp.float32, jnp.bfloat16
    w = _Refs(hb=hb, win=win, wconv=wconv, wpool=wpool, pscale=pscale, gnorm=gnorm, wo=wo, lng=lng, lnb=lnb)
    s = _Refs(zcarry=zcarry, vcarry=vcarry, st=st, q=q, k=k, i=i, g=g, o=o, gate=gate, ymix=ymix, mix=mix)
    first_of_batch = pl.program_id(1) == 0

    @pl.when(first_of_batch & (pl.program_id(0) == 0))
    def _():
        hproj[...] = jnp.dot(x_ref[0].astype(bf16), win[...], preferred_element_type=f32)

    @pl.when(first_of_batch)
    def _():
        zcarry[...] = jnp.zeros(zcarry.shape, f32)
        vcarry[...] = jnp.zeros(vcarry.shape, f32)
        st[...] = jnp.zeros(st.shape, f32)
        mb = meta_ref[0].astype(bf16)
        mout_ref[0] = _mixer_rows(
            layer, N_META, True, meta_ref[0],
            lambda c0, c1: jnp.dot(mb, win[:, c0:c1], preferred_element_type=f32), [], w, s)

    xnb[...] = xnext_ref[0].astype(bf16)

    def piece(c0):
        def run():
            hproj[:, c0:c0 + PROJ_PIECE] = jnp.dot(xnb[...], win[:, c0:c0 + PROJ_PIECE], preferred_element_type=f32)
        return run

    out_ref[0] = _mixer_rows(layer, ROW_TILE, False, x_ref[0], lambda c0, c1: hproj[:, c0:c1],
                             [piece(c0) for c0 in range(0, D_IN, PROJ_PIECE)], w, s)


def _ffn_rows(rows, x, w, s):
    f32, bf16 = jnp.float32, jnp.bfloat16
    s.xb[0:rows] = x.astype(bf16)

    def up_project(j):
        for part in range(2):
            c0 = part * D_FF + j * FF_CHUNK
            s.u[j % FF_AHEAD_SLOTS, 0:rows, part * FF_CHUNK:(part + 1) * FF_CHUNK] = jnp.dot(
                s.xb[0:rows], w.wup[:, c0:c0 + FF_CHUNK], preferred_element_type=f32)

    def conv_rows(j, part):
        c0 = part * D_FF + j * FF_CHUNK
        u = s.u[j % FF_AHEAD_SLOTS, 0:rows, part * FF_CHUNK:(part + 1) * FF_CHUNK]
        prev8 = s.ucarry[j, :, part * FF_CHUNK:(part + 1) * FF_CHUNK]
        s.ucarry[j, :, part * FF_CHUNK:(part + 1) * FF_CHUNK] = u[rows - 8:rows]
        wc = w.wcv[:, c0:c0 + FF_CHUNK]
        return (wc[0:1] * _shifted_rows(u, prev8, 2) + wc[1:2] * _shifted_rows(u, prev8, 1) + wc[2:3] * u
                + w.bcv[:, c0:c0 + FF_CHUNK])

    starts = [sum(FF_GROUPS[:g]) for g in range(len(FF_GROUPS) + 1)]
    group_of = {j: g for g in range(len(FF_GROUPS)) for j in range(starts[g], starts[g + 1])}

    def down_project(grp):
        j0, j1 = starts[grp], starts[grp + 1]
        return jnp.dot(s.a[grp % 2, 0:rows, 0:(j1 - j0) * FF_CHUNK], w.wdn[j0 * FF_CHUNK:j1 * FF_CHUNK, :],
                       preferred_element_type=f32)

    for j in range(FF_AHEAD):
        up_project(j)
    pending = None
    for j in range(N_FF_CHUNKS):
        if j + FF_AHEAD < N_FF_CHUNKS:
            up_project(j + FF_AHEAD)
        if pending is not None:
            if pending == 0:
                s.acc[0:rows] = down_project(pending)
            else:
                s.acc[0:rows] += down_project(pending)
            pending = None
        gate = conv_rows(j, 0)
        val = conv_rows(j, 1)
        grp = group_of[j]
        k = j - starts[grp]
        s.a[grp % 2, 0:rows, k * FF_CHUNK:(k + 1) * FF_CHUNK] = (gate * _sigmoid(gate) * val).astype(bf16)
        if j + 1 == starts[grp + 1] and grp < len(FF_GROUPS) - 1:
            pending = grp
    return _layer_norm(ALPHA * x + s.acc[0:rows] + down_project(len(FF_GROUPS) - 1), w.lng[...], w.lnb[...])


def _ffn_kernel(x_ref, meta_ref, wup, wdn, wcv, bcv, lng, lnb, out_ref, mout_ref,
                xb, u, ucarry, a, acc):
    w = _Refs(wup=wup, wcv=wcv, bcv=bcv, wdn=wdn, lng=lng, lnb=lnb)
    s = _Refs(xb=xb, u=u, ucarry=ucarry, a=a, acc=acc)

    @pl.when(pl.program_id(1) == 0)
    def _():
        ucarry[...] = jnp.zeros(ucarry.shape, jnp.float32)
        mout_ref[0] = _ffn_rows(N_META, meta_ref[0], w, s)

    out_ref[0] = _ffn_rows(FFN_ROW_TILE, x_ref[0], w, s)


def _weight_spec(a, layer):
    if a.ndim == 3:
        return pl.BlockSpec((None,) + a.shape[1:], lambda b, i: (layer, 0, 0), pipeline_mode=pl.Buffered(1))
    zeros = (0,) * a.ndim
    return pl.BlockSpec(a.shape, lambda b, i: zeros, pipeline_mode=pl.Buffered(1))


def _row_spec(tm):
    return pl.BlockSpec((1, tm, D_MODEL), lambda b, i: (b, i, 0))


def _next_row_spec(batch, tiles_per_seq):
    last = batch * tiles_per_seq - 1

    def index(b, i):
        nxt = jnp.minimum(b * tiles_per_seq + i + 1, last)
        return (nxt // tiles_per_seq, nxt % tiles_per_seq, 0)

    return pl.BlockSpec((1, ROW_TILE, D_MODEL), index)


def _meta_spec():
    return pl.BlockSpec((1, N_META, D_MODEL), lambda b, i: (b, 0, 0))


def _compiler_params():
    return pltpu.CompilerParams(dimension_semantics=("arbitrary", "arbitrary"),
                                vmem_limit_bytes=VMEM_LIMIT_BYTES)


def _out_shapes(x, meta):
    return (jax.ShapeDtypeStruct(x.shape, jnp.float32), jax.ShapeDtypeStruct(meta.shape, jnp.float32))


def _mixer_call(layer, x, meta, *weights):
    batch, rows, _ = x.shape
    tm = ROW_TILE
    nc = tm // CHUNK
    f32 = jnp.float32
    return pl.pallas_call(
        functools.partial(_mixer_kernel, layer),
        grid=(batch, rows // tm),
        in_specs=[_row_spec(tm), _next_row_spec(batch, rows // tm), _meta_spec()]
        + [_weight_spec(a, layer) for a in weights],
        out_specs=(_row_spec(tm), _meta_spec()),
        out_shape=_out_shapes(x, meta),
        scratch_shapes=[
            pltpu.VMEM((16, CONV_WIDTH), f32),
            pltpu.VMEM((16, POOL_WIDTH), f32),
            pltpu.VMEM((HG_HEADS, HG_HEAD_DIM, HG_HEAD_DIM), f32),
            pltpu.VMEM((nc, CHUNK, HG_WIDTH), f32),
            pltpu.VMEM((nc, CHUNK, HG_WIDTH), f32),
            pltpu.VMEM((nc, CHUNK, HG_WIDTH), f32),
            pltpu.VMEM((nc, CHUNK, HG_WIDTH), f32),
            pltpu.VMEM((nc, CHUNK, HG_WIDTH), f32),
            pltpu.VMEM((tm, HG_WIDTH), f32),
            pltpu.VMEM((tm, D_MODEL), jnp.bfloat16),
            pltpu.VMEM((tm, D_MODEL), f32),
            pltpu.VMEM((tm, D_IN), f32),
            pltpu.VMEM((tm, D_MODEL), jnp.bfloat16),
        ],
        compiler_params=_compiler_params(),
        name=f"mixer_l{layer}",
    )(x, x, meta, *weights)


def _ffn_call(layer, x, meta, *weights):
    batch, rows, _ = x.shape
    tm = FFN_ROW_TILE
    f32 = jnp.float32
    return pl.pallas_call(
        _ffn_kernel,
        grid=(batch, rows // tm),
        in_specs=[_row_spec(tm), _meta_spec()] + [_weight_spec(a, layer) for a in weights],
        out_specs=(_row_spec(tm), _meta_spec()),
        out_shape=_out_shapes(x, meta),
        scratch_shapes=[
            pltpu.VMEM((tm, D_MODEL), jnp.bfloat16),
            pltpu.VMEM((FF_AHEAD_SLOTS, tm, 2 * FF_CHUNK), f32),
            pltpu.VMEM((N_FF_CHUNKS, 8, 2 * FF_CHUNK), f32),
            pltpu.VMEM((2, tm, max(FF_GROUPS) * FF_CHUNK), jnp.bfloat16),
            pltpu.VMEM((tm, D_MODEL), f32),
        ],
        compiler_params=_compiler_params(),
        name=f"ffn_l{layer}",
    )(x, meta, *weights)


def kernel(x, meta_tokens, hg_lower_bounds, w_in, w_conv, w_pool, pool_scale, hg_norm_g, w_o, ln1_g, ln1_b,
           w_up, w_ffn_conv, b_ffn_conv, w_down, ln2_g, ln2_b):
    bf16 = jnp.bfloat16
    batch, seq, _ = x.shape
    assert seq % ROW_TILE == 0 and ROW_TILE % (CHUNK * CHUNK_UNROLL) == 0 and D_IN % PROJ_PIECE == 0
    assert seq % FFN_ROW_TILE == 0 and sum(FF_GROUPS) == N_FF_CHUNKS
    h = x
    meta = jnp.broadcast_to(meta_tokens[None].astype(x.dtype), (batch, N_META, D_MODEL))
    hb = hg_lower_bounds.astype(jnp.float32)
    w_in_b, w_o_b, w_up_b, w_down_b = (a.astype(bf16) for a in (w_in, w_o, w_up, w_down))
    for l in range(DEPTH):
        wpool_bd = jax.scipy.linalg.block_diag(*[w_pool[l, g] for g in range(len(POOL_WINDOWS))])
        h, meta = _mixer_call(l, h, meta, w_in_b, wpool_bd.astype(bf16), w_o_b, hb, w_conv[l].T,
                              pool_scale[l][None], hg_norm_g[l][None], ln1_g[l][None], ln1_b[l][None])
        h, meta = _ffn_call(l, h, meta, w_up_b, w_down_b, w_ffn_conv[l].T, b_ffn_conv[l][None],
                            ln2_g[l][None], ln2_b[l][None])
    return h
```

```python
import functools

import jax
import jax.numpy as jnp
from jax import lax
from jax.experimental import pallas as pl
from jax.experimental.pallas import tpu as pltpu

D_MODEL = 1024
N_META = 16
CONV_WIDTH = 256
HG_WIDTH = 512
HG_HEAD_DIM = 128
HG_HEADS = 4
POOL_WIDTH = 256
POOL_WINDOWS = (2, 4, 8, 16)
POOL_GROUP_DIM = 64
D_FF = 2816
D_IN = 3 * CONV_WIDTH + 4 * HG_WIDTH + POOL_WIDTH
DEPTH = 2
ALPHA = (2 * DEPTH) ** 0.25
LN_EPS = 1e-5
RMS_EPS = 1e-6
F_FLOOR = 1e-30
LOG2_E = 1.4426950408889634

ROW_TILE = 512
FFN_ROW_TILE = 512
CHUNK = 32
HALF = 8
POOL_HIST = max(POOL_WINDOWS)
CHUNK_UNROLL = 4
PROJ_PIECE = 256
FF_CHUNK = 256
N_FF_CHUNKS = D_FF // FF_CHUNK
FF_GROUPS = (6, 3, 2)
FF_AHEAD = 2
FF_AHEAD_SLOTS = FF_AHEAD + 1
VMEM_LIMIT_BYTES = 56 * 1024 * 1024

_C_CONV = 0
_C_HG = 3 * CONV_WIDTH
_C_POOL = _C_HG + 4 * HG_WIDTH


def _layer_norm(r, g, b):
    mu = jnp.mean(r, axis=-1, keepdims=True)
    c = r - mu
    var = jnp.mean(c * c, axis=-1, keepdims=True)
    return c * lax.rsqrt(var + LN_EPS) * g + b


def _sigmoid(x):
    return 1.0 / (1.0 + jnp.exp(-x))


def _shifted_rows(u, prev8, k):
    hist = jnp.concatenate([prev8, u[0:HALF]], axis=0)
    head = pltpu.roll(hist, k, 0)[HALF:2 * HALF]
    if u.shape[0] == HALF:
        return head
    return jnp.concatenate([head, pltpu.roll(u, k, 0)[HALF:]], axis=0)


def _mixer_rows(layer, rows, is_meta, x, proj, between, w, s):
    f32, bf16 = jnp.float32, jnp.bfloat16
    cl = min(CHUNK, rows)
    nc = rows // cl
    tiles = cl // HALF
    between = list(between)

    hh = proj(_C_HG, _C_POOL)
    hq = hh[:, 0:HG_WIDTH]
    fz = hh[:, HG_WIDTH:2 * HG_WIDTH]
    hi = hh[:, 2 * HG_WIDTH:3 * HG_WIDTH]
    gz = hh[:, 3 * HG_WIDTH:4 * HG_WIDTH]

    hb = w.hb[...]
    e = jnp.exp(hb - jnp.max(hb, axis=0, keepdims=True))
    p = e / jnp.sum(e, axis=0, keepdims=True)
    lb = jnp.sum(p[0:layer + 1], axis=0, keepdims=True) - p[0:1]

    sig = _sigmoid(fz)
    f = lb + (1.0 - lb) * sig
    g = (jnp.log(jnp.maximum(f, F_FLOOR)) * LOG2_E).reshape(rows // HALF, HALF, HG_WIDTH)
    rid3 = lax.broadcasted_iota(jnp.int32, g.shape, 1)
    sh = 1
    while sh < HALF:
        g = g + jnp.where(rid3 >= sh, pltpu.roll(g, sh, 1), 0.0)
        sh *= 2
    g = g.reshape(nc, tiles, HALF, HG_WIDTH)
    run = g[:, 0]
    s.g[0:nc, 0:HALF, :] = run
    for j in range(1, tiles):
        run = g[:, j] + run[:, HALF - 1:HALF, :]
        s.g[0:nc, j * HALF:(j + 1) * HALF, :] = run
    s.q[0:nc, 0:cl] = (hq * (HG_HEAD_DIM ** -0.5)).reshape(nc, cl, HG_WIDTH)
    s.k[0:nc, 0:cl] = ((1.0 - lb) * (1.0 - sig)).reshape(nc, cl, HG_WIDTH)
    s.i[0:nc, 0:cl] = hi.reshape(nc, cl, HG_WIDTH)
    s.gate[0:rows] = gz * _sigmoid(gz)

    hc = proj(_C_CONV, _C_HG)
    cb = hc[:, 0:CONV_WIDTH]
    z = hc[:, CONV_WIDTH:2 * CONV_WIDTH] * hc[:, 2 * CONV_WIDTH:3 * CONV_WIDTH]
    zprev = s.zcarry[0:HALF]
    s.zcarry[0:HALF] = z[rows - HALF:rows]
    wc = w.wconv[...]
    conv = wc[0:1] * _shifted_rows(z, zprev, 2) + wc[1:2] * _shifted_rows(z, zprev, 1) + wc[2:3] * z
    s.ymix[0:rows, 0:CONV_WIDTH] = (cb * conv).astype(bf16)

    pv = proj(_C_POOL, D_IN)
    ext = jnp.concatenate([s.vcarry[...], pv], axis=0)
    s.vcarry[...] = pv[rows - POOL_HIST:rows]
    sums = {1: ext}
    width = 1
    while width < POOL_HIST:
        sums[2 * width] = sums[width] + pltpu.roll(sums[width], width, 0)
        width *= 2
    lane = lax.broadcasted_iota(jnp.int32, (1, POOL_WIDTH), 1)
    win = jnp.full((1, POOL_WIDTH), POOL_WINDOWS[-1], f32)
    wsum = sums[POOL_WINDOWS[-1]][POOL_HIST:]
    for gi in reversed(range(len(POOL_WINDOWS) - 1)):
        in_lower_groups = lane < (gi + 1) * POOL_GROUP_DIM
        win = jnp.where(in_lower_groups, float(POOL_WINDOWS[gi]), win)
        wsum = jnp.where(in_lower_groups, sums[POOL_WINDOWS[gi]][POOL_HIST:], wsum)
    if is_meta:
        pos = lax.broadcasted_iota(jnp.int32, (rows, 1), 0).astype(f32)
        count = jnp.minimum(pos + 1.0, win)
    else:
        count = win
    dpool = wsum / count - pv
    ypool = jnp.dot(dpool.astype(bf16), w.wpool[...], preferred_element_type=f32) * w.pscale[...]
    s.ymix[0:rows, CONV_WIDTH + HG_WIDTH:D_MODEL] = ypool.astype(bf16)

    s.mix[0:rows] = (
        jnp.dot(s.ymix[0:rows, 0:CONV_WIDTH], w.wo[0:CONV_WIDTH, :], preferred_element_type=f32)
        + jnp.dot(s.ymix[0:rows, CONV_WIDTH + HG_WIDTH:D_MODEL], w.wo[CONV_WIDTH + HG_WIDTH:D_MODEL, :],
                  preferred_element_type=f32))

    rid = lax.broadcasted_iota(jnp.int32, (HALF, 1), 0)
    sgn4 = jnp.where(rid < 4, -1.0, 1.0).astype(f32)
    sgn2 = jnp.where((rid & 2) == 0, -1.0, 1.0).astype(f32)
    odd = (rid & 1) == 1
    lo4 = rid < 4
    nst = HG_HEADS * cl
    tt = lax.broadcasted_iota(jnp.int32, (nst, nst), 0)
    uu = lax.broadcasted_iota(jnp.int32, (nst, nst), 1)
    levels = [1 << b for b in range(cl.bit_length() - 1)]
    masks = [((tt // (2 * m)) == (uu // (2 * m))) & ((tt % (2 * m)) >= m) & ((uu % (2 * m)) < m)
             for m in levels]
    diag = tt == uu
    contract_lanes = (((1,), (1,)), ((), ()))
    head_lanes = [slice(h * HG_HEAD_DIM, (h + 1) * HG_HEAD_DIM) for h in range(HG_HEADS)]

    def stack_heads(v):
        return jnp.concatenate([v[:, ls] for ls in head_lanes], axis=0)

    def level_exponents(c, g_tiles, m):
        out = []
        for j, gt in enumerate(g_tiles):
            r0 = j * HALF
            if m >= HALF:
                ref = (r0 // (2 * m)) * 2 * m + m - 1
                gref = s.g[c, ref:ref + 1, :]
                out.append(gref - gt if (r0 % (2 * m)) < m else gt - gref)
            elif m == 4:
                out.append(sgn4 * (gt - s.g[c, r0 + 3:r0 + 4, :]))
            elif m == 2:
                out.append(sgn2 * (gt - jnp.where(lo4, s.g[c, r0 + 1:r0 + 2, :], s.g[c, r0 + 5:r0 + 6, :])))
            else:
                out.append(jnp.where(odd, gt - pltpu.roll(gt, 1, 0), 0.0))
        return jnp.concatenate(out, axis=0)

    def emit_between():
        if between:
            between.pop(0)()

    def process_chunks(chunks):
        gcs = [s.g[c, 0:cl, :] for c in chunks]
        ibs = [s.i[c, 0:cl, :].astype(bf16) for c in chunks]
        states = [[s.st[h]] for h in range(HG_HEADS)]
        for n, c in enumerate(chunks):
            gl = gcs[n][cl - 1:cl]
            kt = (s.k[c, 0:cl, :] * jnp.exp2(gl - gcs[n])).astype(bf16)
            decay = jnp.exp2(gl)
            for h, ls in enumerate(head_lanes):
                delta = lax.dot_general(ibs[n][:, ls], kt[:, ls], (((0,), (0,)), ((), ())),
                                        preferred_element_type=f32)
                states[h].append(decay[:, ls] * states[h][-1] + delta)
        for h in range(HG_HEADS):
            s.st[h] = states[h][-1]
        emit_between()
        amats = []
        for n, c in enumerate(chunks):
            g_tiles = [gcs[n][j * HALF:(j + 1) * HALF] for j in range(tiles)]
            qb = s.q[c, 0:cl, :].astype(bf16)
            kb = s.k[c, 0:cl, :].astype(bf16)
            a = jnp.where(diag, lax.dot_general(stack_heads(qb), stack_heads(kb), contract_lanes,
                                                preferred_element_type=f32), 0.0)
            for m, mask in zip(levels, masks):
                eb = jnp.exp2(level_exponents(c, g_tiles, m)).astype(bf16)
                a = jnp.where(mask, lax.dot_general(stack_heads(qb * eb), stack_heads(kb * eb), contract_lanes,
                                                    preferred_element_type=f32), a)
            amats.append(a.astype(bf16))
        emit_between()
        inters = []
        for n, c in enumerate(chunks):
            qt = (s.q[c, 0:cl, :] * jnp.exp2(gcs[n])).astype(bf16)
            inters.append([jnp.dot(qt[:, ls], states[h][n].T.astype(bf16), preferred_element_type=f32)
                           for h, ls in enumerate(head_lanes)])
        emit_between()
        for n, c in enumerate(chunks):
            intra = jnp.dot(amats[n], stack_heads(ibs[n]), preferred_element_type=f32)
            for h, ls in enumerate(head_lanes):
                s.o[c, 0:cl, ls] = inters[n][h] + intra[h * cl:(h + 1) * cl]

    for c0 in range(0, nc, CHUNK_UNROLL):
        process_chunks(list(range(c0, min(c0 + CHUNK_UNROLL, nc))))
    while between:
        emit_between()

    gn = w.gnorm[...]
    for h in range(HG_HEADS):
        ls = slice(h * HG_HEAD_DIM, (h + 1) * HG_HEAD_DIM)
        o = s.o[0:nc, 0:cl, ls].reshape(rows, HG_HEAD_DIM)
        o = o * lax.rsqrt(jnp.mean(o * o, axis=-1, keepdims=True) + RMS_EPS) * gn
        o = o * s.gate[0:rows, ls]
        s.ymix[0:rows, CONV_WIDTH + h * HG_HEAD_DIM:CONV_WIDTH + (h + 1) * HG_HEAD_DIM] = o.astype(bf16)

    mix = s.mix[0:rows] + jnp.dot(s.ymix[0:rows, CONV_WIDTH:CONV_WIDTH + HG_WIDTH],
                                  w.wo[CONV_WIDTH:CONV_WIDTH + HG_WIDTH, :], preferred_element_type=f32)
    return _layer_norm(ALPHA * x + mix, w.lng[...], w.lnb[...])


class _Refs:
    def __init__(self, **kw):
        self.__dict__.update(kw)


def _mixer_kernel(layer, x_ref, xnext_ref, meta_ref, win, wpool, wo, hb, wconv, pscale, gnorm, lng, lnb,
                  out_ref, mout_ref,
                  zcarry, vcarry, st, q, k, i, g, o, gate, ymix, mix, hproj, xnb):
    f32, bf16 = jnp.float32, jnp.bfloat16
    w = _Refs(hb=hb, win=win, wconv=wconv, wpool=wpool, pscale=pscale, gnorm=gnorm, wo=wo, lng=lng, lnb=lnb)
    s = _Refs(zcarry=zcarry, vcarry=vcarry, st=st, q=q, k=k, i=i, g=g, o=o, gate=gate, ymix=ymix, mix=mix)
    first_of_batch = pl.program_id(1) == 0

    @pl.when(first_of_batch & (pl.program_id(0) == 0))
    def _():
        hproj[...] = jnp.dot(x_ref[0].astype(bf16), win[...], preferred_element_type=f32)

    @pl.when(first_of_batch)
    def _():
        zcarry[...] = jnp.zeros(zcarry.shape, f32)
        vcarry[...] = jnp.zeros(vcarry.shape, f32)
        st[...] = jnp.zeros(st.shape, f32)
        mb = meta_ref[0].astype(bf16)
        mout_ref[0] = _mixer_rows(
            layer, N_META, True, meta_ref[0],
            lambda c0, c1: jnp.dot(mb, win[:, c0:c1], preferred_element_type=f32), [], w, s)

    xnb[...] = xnext_ref[0].astype(bf16)

    def piece(c0):
        def run():
            hproj[:, c0:c0 + PROJ_PIECE] = jnp.dot(xnb[...], win[:, c0:c0 + PROJ_PIECE], preferred_element_type=f32)
        return run

    out_ref[0] = _mixer_rows(layer, ROW_TILE, False, x_ref[0], lambda c0, c1: hproj[:, c0:c1],
                             [piece(c0) for c0 in range(0, D_IN, PROJ_PIECE)], w, s)


def _ffn_rows(rows, x, w, s):
    f32, bf16 = jnp.float32, jnp.bfloat16
    s.xb[0:rows] = x.astype(bf16)

    def up_project(j):
        for part in range(2):
            c0 = part * D_FF + j * FF_CHUNK
            s.u[j % FF_AHEAD_SLOTS, 0:rows, part * FF_CHUNK:(part + 1) * FF_CHUNK] = jnp.dot(
                s.xb[0:rows], w.wup[:, c0:c0 + FF_CHUNK], preferred_element_type=f32)

    def conv_rows(j, part):
        c0 = part * D_FF + j * FF_CHUNK
        u = s.u[j % FF_AHEAD_SLOTS, 0:rows, part * FF_CHUNK:(part + 1) * FF_CHUNK]
        prev8 = s.ucarry[j, :, part * FF_CHUNK:(part + 1) * FF_CHUNK]
        s.ucarry[j, :, part * FF_CHUNK:(part + 1) * FF_CHUNK] = u[rows - HALF:rows]
        wc = w.wcv[:, c0:c0 + FF_CHUNK]
        return (wc[0:1] * _shifted_rows(u, prev8, 2) + wc[1:2] * _shifted_rows(u, prev8, 1) + wc[2:3] * u
                + w.bcv[:, c0:c0 + FF_CHUNK])

    starts = [sum(FF_GROUPS[:g]) for g in range(len(FF_GROUPS) + 1)]
    group_of = {j: g for g in range(len(FF_GROUPS)) for j in range(starts[g], starts[g + 1])}

    def down_project(grp):
        j0, j1 = starts[grp], starts[grp + 1]
        return jnp.dot(s.a[grp % 2, 0:rows, 0:(j1 - j0) * FF_CHUNK], w.wdn[j0 * FF_CHUNK:j1 * FF_CHUNK, :],
                       preferred_element_type=f32)

    for j in range(FF_AHEAD):
        up_project(j)
    pending = None
    for j in range(N_FF_CHUNKS):
        if j + FF_AHEAD < N_FF_CHUNKS:
            up_project(j + FF_AHEAD)
        if pending is not None:
            if pending == 0:
                s.acc[0:rows] = down_project(pending)
            else:
                s.acc[0:rows] += down_project(pending)
            pending = None
        gate = conv_rows(j, 0)
        val = conv_rows(j, 1)
        grp = group_of[j]
        k = j - starts[grp]
        s.a[grp % 2, 0:rows, k * FF_CHUNK:(k + 1) * FF_CHUNK] = (gate * _sigmoid(gate) * val).astype(bf16)
        if j + 1 == starts[grp + 1] and grp < len(FF_GROUPS) - 1:
            pending = grp
    return _layer_norm(ALPHA * x + s.acc[0:rows] + down_project(len(FF_GROUPS) - 1), w.lng[...], w.lnb[...])


def _ffn_kernel(x_ref, meta_ref, wup, wdn, wcv, bcv, lng, lnb, out_ref, mout_ref,
                xb, u, ucarry, a, acc):
    w = _Refs(wup=wup, wcv=wcv, bcv=bcv, wdn=wdn, lng=lng, lnb=lnb)
    s = _Refs(xb=xb, u=u, ucarry=ucarry, a=a, acc=acc)

    @pl.when(pl.program_id(1) == 0)
    def _():
        ucarry[...] = jnp.zeros(ucarry.shape, jnp.float32)
        mout_ref[0] = _ffn_rows(N_META, meta_ref[0], w, s)

    out_ref[0] = _ffn_rows(FFN_ROW_TILE, x_ref[0], w, s)


def _weight_spec(a, layer):
    if a.ndim == 3:
        return pl.BlockSpec((None,) + a.shape[1:], lambda b, i: (layer, 0, 0), pipeline_mode=pl.Buffered(1))
    zeros = (0,) * a.ndim
    return pl.BlockSpec(a.shape, lambda b, i: zeros, pipeline_mode=pl.Buffered(1))


def _row_spec(tm):
    return pl.BlockSpec((1, tm, D_MODEL), lambda b, i: (b, i, 0))


def _next_row_spec(batch, tiles_per_seq):
    last = batch * tiles_per_seq - 1

    def index(b, i):
        nxt = jnp.minimum(b * tiles_per_seq + i + 1, last)
        return (nxt // tiles_per_seq, nxt % tiles_per_seq, 0)

    return pl.BlockSpec((1, ROW_TILE, D_MODEL), index)


def _meta_spec():
    return pl.BlockSpec((1, N_META, D_MODEL), lambda b, i: (b, 0, 0))


def _compiler_params():
    return pltpu.CompilerParams(dimension_semantics=("arbitrary", "arbitrary"),
                                vmem_limit_bytes=VMEM_LIMIT_BYTES)


def _out_shapes(x, meta):
    return (jax.ShapeDtypeStruct(x.shape, jnp.float32), jax.ShapeDtypeStruct(meta.shape, jnp.float32))


def _mixer_call(layer, x, meta, *weights):
    batch, rows, _ = x.shape
    tm = ROW_TILE
    nc = tm // CHUNK
    f32 = jnp.float32
    return pl.pallas_call(
        functools.partial(_mixer_kernel, layer),
        grid=(batch, rows // tm),
        in_specs=[_row_spec(tm), _next_row_spec(batch, rows // tm), _meta_spec()]
        + [_weight_spec(a, layer) for a in weights],
        out_specs=(_row_spec(tm), _meta_spec()),
        out_shape=_out_shapes(x, meta),
        scratch_shapes=[
            pltpu.VMEM((2 * HALF, CONV_WIDTH), f32),
            pltpu.VMEM((POOL_HIST, POOL_WIDTH), f32),
            pltpu.VMEM((HG_HEADS, HG_HEAD_DIM, HG_HEAD_DIM), f32),
            pltpu.VMEM((nc, CHUNK, HG_WIDTH), f32),
            pltpu.VMEM((nc, CHUNK, HG_WIDTH), f32),
            pltpu.VMEM((nc, CHUNK, HG_WIDTH), f32),
            pltpu.VMEM((nc, CHUNK, HG_WIDTH), f32),
            pltpu.VMEM((nc, CHUNK, HG_WIDTH), f32),
            pltpu.VMEM((tm, HG_WIDTH), f32),
            pltpu.VMEM((tm, D_MODEL), jnp.bfloat16),
            pltpu.VMEM((tm, D_MODEL), f32),
            pltpu.VMEM((tm, D_IN), f32),
            pltpu.VMEM((tm, D_MODEL), jnp.bfloat16),
        ],
        compiler_params=_compiler_params(),
        name=f"mixer_l{layer}",
    )(x, x, meta, *weights)


def _ffn_call(layer, x, meta, *weights):
    batch, rows, _ = x.shape
    tm = FFN_ROW_TILE
    f32 = jnp.float32
    return pl.pallas_call(
        _ffn_kernel,
        grid=(batch, rows // tm),
        in_specs=[_row_spec(tm), _meta_spec()] + [_weight_spec(a, layer) for a in weights],
        out_specs=(_row_spec(tm), _meta_spec()),
        out_shape=_out_shapes(x, meta),
        scratch_shapes=[
            pltpu.VMEM((tm, D_MODEL), jnp.bfloat16),
            pltpu.VMEM((FF_AHEAD_SLOTS, tm, 2 * FF_CHUNK), f32),
            pltpu.VMEM((N_FF_CHUNKS, HALF, 2 * FF_CHUNK), f32),
            pltpu.VMEM((2, tm, max(FF_GROUPS) * FF_CHUNK), jnp.bfloat16),
            pltpu.VMEM((tm, D_MODEL), f32),
        ],
        compiler_params=_compiler_params(),
        name=f"ffn_l{layer}",
    )(x, meta, *weights)


def kernel(x, meta_tokens, hg_lower_bounds, w_in, w_conv, w_pool, pool_scale, hg_norm_g, w_o, ln1_g, ln1_b,
           w_up, w_ffn_conv, b_ffn_conv, w_down, ln2_g, ln2_b):
    bf16 = jnp.bfloat16
    batch, seq, _ = x.shape
    assert seq % ROW_TILE == 0 and ROW_TILE % (CHUNK * CHUNK_UNROLL) == 0 and D_IN % PROJ_PIECE == 0
    assert seq % FFN_ROW_TILE == 0 and sum(FF_GROUPS) == N_FF_CHUNKS
    assert POOL_HIST <= N_META <= CHUNK and N_META % (2 * HALF) == 0 and POOL_HIST % HALF == 0
    h = x
    meta = jnp.broadcast_to(meta_tokens[None].astype(x.dtype), (batch, N_META, D_MODEL))
    hb = hg_lower_bounds.astype(jnp.float32)
    w_in_b, w_o_b, w_up_b, w_down_b = (a.astype(bf16) for a in (w_in, w_o, w_up, w_down))
    for l in range(DEPTH):
        wpool_bd = jax.scipy.linalg.block_diag(*[w_pool[l, g] for g in range(len(POOL_WINDOWS))])
        h, meta = _mixer_call(l, h, meta, w_in_b, wpool_bd.astype(bf16), w_o_b, hb, w_conv[l].T,
                              pool_scale[l][None], hg_norm_g[l][None], ln1_g[l][None], ln1_b[l][None])
        h, meta = _ffn_call(l, h, meta, w_up_b, w_down_b, w_ffn_conv[l].T, b_ffn_conv[l][None],
                            ln2_g[l][None], ln2_b[l][None])
    return h
```

```python
import functools

import jax
import jax.numpy as jnp
from jax import lax
from jax.experimental import pallas as pl
from jax.experimental.pallas import tpu as pltpu

D_MODEL = 1024
N_META = 16
CONV_WIDTH = 256
HG_WIDTH = 512
HG_HEAD_DIM = 128
HG_HEADS = 4
POOL_WIDTH = 256
POOL_WINDOWS = (2, 4, 8, 16)
POOL_GROUP_DIM = 64
D_FF = 2816
D_IN = 3 * CONV_WIDTH + 4 * HG_WIDTH + POOL_WIDTH
DEPTH = 2
ALPHA = (2 * DEPTH) ** 0.25
LN_EPS = 1e-5
RMS_EPS = 1e-6
F_FLOOR = 1e-30
LOG2_E = 1.4426950408889634

ROW_TILE = 512
FFN_ROW_TILE = 512
CHUNK = 32
HALF = 8
POOL_HIST = max(POOL_WINDOWS)
CHUNK_UNROLL = 4
PROJ_PIECE = 256
FF_CHUNK = 256
N_FF_CHUNKS = D_FF // FF_CHUNK
FF_GROUPS = (6, 3, 2)
FF_AHEAD = 2
FF_AHEAD_SLOTS = FF_AHEAD + 1
VMEM_LIMIT_BYTES = 56 * 1024 * 1024

_C_CONV = 0
_C_HG = 3 * CONV_WIDTH
_C_POOL = _C_HG + 4 * HG_WIDTH


def _layer_norm(r, g, b):
    mu = jnp.mean(r, axis=-1, keepdims=True)
    c = r - mu
    var = jnp.mean(c * c, axis=-1, keepdims=True)
    return c * lax.rsqrt(var + LN_EPS) * g + b


def _sigmoid(x):
    return 1.0 / (1.0 + jnp.exp(-x))


def _shifted_rows(u, prev8, k):
    hist = jnp.concatenate([prev8, u[0:HALF]], axis=0)
    head = pltpu.roll(hist, k, 0)[HALF:2 * HALF]
    if u.shape[0] == HALF:
        return head
    return jnp.concatenate([head, pltpu.roll(u, k, 0)[HALF:]], axis=0)


def _mixer_rows(layer, rows, is_meta, x, proj, between, w, s):
    f32, bf16 = jnp.float32, jnp.bfloat16
    cl = min(CHUNK, rows)
    nc = rows // cl
    tiles = cl // HALF
    between = list(between)

    hh = proj(_C_HG, _C_POOL)
    hq = hh[:, 0:HG_WIDTH]
    fz = hh[:, HG_WIDTH:2 * HG_WIDTH]
    hi = hh[:, 2 * HG_WIDTH:3 * HG_WIDTH]
    gz = hh[:, 3 * HG_WIDTH:4 * HG_WIDTH]

    hb = w.hb[...]
    e = jnp.exp(hb - jnp.max(hb, axis=0, keepdims=True))
    p = e / jnp.sum(e, axis=0, keepdims=True)
    lb = jnp.sum(p[0:layer + 1], axis=0, keepdims=True) - p[0:1]

    sig = _sigmoid(fz)
    f = lb + (1.0 - lb) * sig
    g = (jnp.log(jnp.maximum(f, F_FLOOR)) * LOG2_E).reshape(rows // HALF, HALF, HG_WIDTH)
    rid3 = lax.broadcasted_iota(jnp.int32, g.shape, 1)
    sh = 1
    while sh < HALF:
        g = g + jnp.where(rid3 >= sh, pltpu.roll(g, sh, 1), 0.0)
        sh *= 2
    g = g.reshape(nc, tiles, HALF, HG_WIDTH)
    run = g[:, 0]
    s.g[0:nc, 0:HALF, :] = run
    for j in range(1, tiles):
        run = g[:, j] + run[:, HALF - 1:HALF, :]
        s.g[0:nc, j * HALF:(j + 1) * HALF, :] = run
    s.q[0:nc, 0:cl] = (hq * (HG_HEAD_DIM ** -0.5)).reshape(nc, cl, HG_WIDTH)
    s.k[0:nc, 0:cl] = ((1.0 - lb) * (1.0 - sig)).reshape(nc, cl, HG_WIDTH)
    s.i[0:nc, 0:cl] = hi.reshape(nc, cl, HG_WIDTH)
    s.gate[0:rows] = gz * _sigmoid(gz)

    hc = proj(_C_CONV, _C_HG)
    cb = hc[:, 0:CONV_WIDTH]
    z = hc[:, CONV_WIDTH:2 * CONV_WIDTH] * hc[:, 2 * CONV_WIDTH:3 * CONV_WIDTH]
    zprev = s.zcarry[0:HALF]
    s.zcarry[0:HALF] = z[rows - HALF:rows]
    wc = w.wconv[...]
    conv = wc[0:1] * _shifted_rows(z, zprev, 2) + wc[1:2] * _shifted_rows(z, zprev, 1) + wc[2:3] * z
    s.ymix[0:rows, 0:CONV_WIDTH] = (cb * conv).astype(bf16)

    pv = proj(_C_POOL, D_IN)
    ext = jnp.concatenate([s.vcarry[...], pv], axis=0)
    s.vcarry[...] = pv[rows - POOL_HIST:rows]
    sums = {1: ext}
    width = 1
    while width < POOL_HIST:
        sums[2 * width] = sums[width] + pltpu.roll(sums[width], width, 0)
        width *= 2
    lane = lax.broadcasted_iota(jnp.int32, (1, POOL_WIDTH), 1)
    win = jnp.full((1, POOL_WIDTH), POOL_WINDOWS[-1], f32)
    wsum = sums[POOL_WINDOWS[-1]][POOL_HIST:]
    for gi in reversed(range(len(POOL_WINDOWS) - 1)):
        in_lower_groups = lane < (gi + 1) * POOL_GROUP_DIM
        win = jnp.where(in_lower_groups, float(POOL_WINDOWS[gi]), win)
        wsum = jnp.where(in_lower_groups, sums[POOL_WINDOWS[gi]][POOL_HIST:], wsum)
    if is_meta:
        pos = lax.broadcasted_iota(jnp.int32, (rows, 1), 0).astype(f32)
        count = jnp.minimum(pos + 1.0, win)
    else:
        count = win
    dpool = wsum / count - pv
    ypool = jnp.dot(dpool.astype(bf16), w.wpool[...], preferred_element_type=f32) * w.pscale[...]
    s.ymix[0:rows, CONV_WIDTH + HG_WIDTH:D_MODEL] = ypool.astype(bf16)

    s.mix[0:rows] = (
        jnp.dot(s.ymix[0:rows, 0:CONV_WIDTH], w.wo[0:CONV_WIDTH, :], preferred_element_type=f32)
        + jnp.dot(s.ymix[0:rows, CONV_WIDTH + HG_WIDTH:D_MODEL], w.wo[CONV_WIDTH + HG_WIDTH:D_MODEL, :],
                  preferred_element_type=f32))

    rid = lax.broadcasted_iota(jnp.int32, (HALF, 1), 0)
    sgn4 = jnp.where(rid < 4, -1.0, 1.0).astype(f32)
    sgn2 = jnp.where((rid & 2) == 0, -1.0, 1.0).astype(f32)
    odd = (rid & 1) == 1
    lo4 = rid < 4
    nst = HG_HEADS * cl
    tt = lax.broadcasted_iota(jnp.int32, (nst, nst), 0)
    uu = lax.broadcasted_iota(jnp.int32, (nst, nst), 1)
    levels = [1 << b for b in range(cl.bit_length() - 1)]
    masks = [((tt // (2 * m)) == (uu // (2 * m))) & ((tt % (2 * m)) >= m) & ((uu % (2 * m)) < m)
             for m in levels]
    diag = tt == uu
    contract_lanes = (((1,), (1,)), ((), ()))
    head_lanes = [slice(h * HG_HEAD_DIM, (h + 1) * HG_HEAD_DIM) for h in range(HG_HEADS)]

    def stack_heads(v):
        return jnp.concatenate([v[:, ls] for ls in head_lanes], axis=0)

    def level_exponents(c, g_tiles, m):
        out = []
        for j, gt in enumerate(g_tiles):
            r0 = j * HALF
            if m >= HALF:
                ref = (r0 // (2 * m)) * 2 * m + m - 1
                gref = s.g[c, ref:ref + 1, :]
                out.append(gref - gt if (r0 % (2 * m)) < m else gt - gref)
            elif m == 4:
                out.append(sgn4 * (gt - s.g[c, r0 + 3:r0 + 4, :]))
            elif m == 2:
                out.append(sgn2 * (gt - jnp.where(lo4, s.g[c, r0 + 1:r0 + 2, :], s.g[c, r0 + 5:r0 + 6, :])))
            else:
                out.append(jnp.where(odd, gt - pltpu.roll(gt, 1, 0), 0.0))
        return jnp.concatenate(out, axis=0)

    def emit_between():
        if between:
            between.pop(0)()

    def process_chunks(chunks):
        gcs = [s.g[c, 0:cl, :] for c in chunks]
        ibs = [s.i[c, 0:cl, :].astype(bf16) for c in chunks]
        states = [[s.st[h]] for h in range(HG_HEADS)]
        for n, c in enumerate(chunks):
            gl = gcs[n][cl - 1:cl]
            kt = (s.k[c, 0:cl, :] * jnp.exp2(gl - gcs[n])).astype(bf16)
            decay = jnp.exp2(gl)
            for h, ls in enumerate(head_lanes):
                delta = lax.dot_general(ibs[n][:, ls], kt[:, ls], (((0,), (0,)), ((), ())),
                                        preferred_element_type=f32)
                states[h].append(decay[:, ls] * states[h][-1] + delta)
        for h in range(HG_HEADS):
            s.st[h] = states[h][-1]
        emit_between()
        amats = []
        for n, c in enumerate(chunks):
            g_tiles = [gcs[n][j * HALF:(j + 1) * HALF] for j in range(tiles)]
            qb = s.q[c, 0:cl, :].astype(bf16)
            kf = s.k[c, 0:cl, :]
            a = jnp.where(diag, jnp.dot(stack_heads(qb), stack_heads(kf).T.astype(bf16),
                                        preferred_element_type=f32), 0.0)
            for m, mask in zip(levels, masks):
                ef = jnp.exp2(level_exponents(c, g_tiles, m))
                a = jnp.where(mask, jnp.dot(stack_heads(qb * ef.astype(bf16)), stack_heads(kf * ef).T.astype(bf16),
                                            preferred_element_type=f32), a)
            amats.append(a.astype(bf16))
        emit_between()
        inters = []
        for n, c in enumerate(chunks):
            qt = (s.q[c, 0:cl, :] * jnp.exp2(gcs[n])).astype(bf16)
            inters.append([jnp.dot(qt[:, ls], states[h][n].T.astype(bf16), preferred_element_type=f32)
                           for h, ls in enumerate(head_lanes)])
        emit_between()
        for n, c in enumerate(chunks):
            intra = jnp.dot(amats[n], stack_heads(ibs[n]), preferred_element_type=f32)
            for h, ls in enumerate(head_lanes):
                s.o[c, 0:cl, ls] = inters[n][h] + intra[h * cl:(h + 1) * cl]

    for c0 in range(0, nc, CHUNK_UNROLL):
        process_chunks(list(range(c0, min(c0 + CHUNK_UNROLL, nc))))
    while between:
        emit_between()

    gn = w.gnorm[...]
    for h in range(HG_HEADS):
        ls = slice(h * HG_HEAD_DIM, (h + 1) * HG_HEAD_DIM)
        o = s.o[0:nc, 0:cl, ls].reshape(rows, HG_HEAD_DIM)
        o = o * lax.rsqrt(jnp.mean(o * o, axis=-1, keepdims=True) + RMS_EPS) * gn
        o = o * s.gate[0:rows, ls]
        s.ymix[0:rows, CONV_WIDTH + h * HG_HEAD_DIM:CONV_WIDTH + (h + 1) * HG_HEAD_DIM] = o.astype(bf16)

    mix = s.mix[0:rows] + jnp.dot(s.ymix[0:rows, CONV_WIDTH:CONV_WIDTH + HG_WIDTH],
                                  w.wo[CONV_WIDTH:CONV_WIDTH + HG_WIDTH, :], preferred_element_type=f32)
    return _layer_norm(ALPHA * x + mix, w.lng[...], w.lnb[...])


class _Refs:
    def __init__(self, **kw):
        self.__dict__.update(kw)


def _mixer_kernel(layer, x_ref, xnext_ref, meta_ref, win, wpool, wo, hb, wconv, pscale, gnorm, lng, lnb,
                  out_ref, mout_ref,
                  zcarry, vcarry, st, q, k, i, g, o, gate, ymix, mix, hproj, xnb):
    f32, bf16 = jnp.float32, jnp.bfloat16
    w = _Refs(hb=hb, win=win, wconv=wconv, wpool=wpool, pscale=pscale, gnorm=gnorm, wo=wo, lng=lng, lnb=lnb)
    s = _Refs(zcarry=zcarry, vcarry=vcarry, st=st, q=q, k=k, i=i, g=g, o=o, gate=gate, ymix=ymix, mix=mix)
    first_of_batch = pl.program_id(1) == 0

    @pl.when(first_of_batch & (pl.program_id(0) == 0))
    def _():
        hproj[...] = jnp.dot(x_ref[0].astype(bf16), win[...], preferred_element_type=f32)

    @pl.when(first_of_batch)
    def _():
        zcarry[...] = jnp.zeros(zcarry.shape, f32)
        vcarry[...] = jnp.zeros(vcarry.shape, f32)
        st[...] = jnp.zeros(st.shape, f32)
        mb = meta_ref[0].astype(bf16)
        mout_ref[0] = _mixer_rows(
            layer, N_META, True, meta_ref[0],
            lambda c0, c1: jnp.dot(mb, win[:, c0:c1], preferred_element_type=f32), [], w, s)

    xnb[...] = xnext_ref[0].astype(bf16)

    def piece(c0):
        def run():
            hproj[:, c0:c0 + PROJ_PIECE] = jnp.dot(xnb[...], win[:, c0:c0 + PROJ_PIECE], preferred_element_type=f32)
        return run

    out_ref[0] = _mixer_rows(layer, ROW_TILE, False, x_ref[0], lambda c0, c1: hproj[:, c0:c1],
                             [piece(c0) for c0 in range(0, D_IN, PROJ_PIECE)], w, s)


def _ffn_rows(rows, x, w, s):
    f32, bf16 = jnp.float32, jnp.bfloat16
    s.xb[0:rows] = x.astype(bf16)

    def up_project(j):
        for part in range(2):
            c0 = part * D_FF + j * FF_CHUNK
            s.u[j % FF_AHEAD_SLOTS, 0:rows, part * FF_CHUNK:(part + 1) * FF_CHUNK] = jnp.dot(
                s.xb[0:rows], w.wup[:, c0:c0 + FF_CHUNK], preferred_element_type=f32)

    def conv_rows(j, part):
        c0 = part * D_FF + j * FF_CHUNK
        u = s.u[j % FF_AHEAD_SLOTS, 0:rows, part * FF_CHUNK:(part + 1) * FF_CHUNK]
        prev8 = s.ucarry[j, :, part * FF_CHUNK:(part + 1) * FF_CHUNK]
        s.ucarry[j, :, part * FF_CHUNK:(part + 1) * FF_CHUNK] = u[rows - HALF:rows]
        wc = w.wcv[:, c0:c0 + FF_CHUNK]
        return (wc[0:1] * _shifted_rows(u, prev8, 2) + wc[1:2] * _shifted_rows(u, prev8, 1) + wc[2:3] * u
                + w.bcv[:, c0:c0 + FF_CHUNK])

    starts = [sum(FF_GROUPS[:g]) for g in range(len(FF_GROUPS) + 1)]
    group_of = {j: g for g in range(len(FF_GROUPS)) for j in range(starts[g], starts[g + 1])}

    def down_project(grp):
        j0, j1 = starts[grp], starts[grp + 1]
        return jnp.dot(s.a[grp % 2, 0:rows, 0:(j1 - j0) * FF_CHUNK], w.wdn[j0 * FF_CHUNK:j1 * FF_CHUNK, :],
                       preferred_element_type=f32)

    for j in range(FF_AHEAD):
        up_project(j)
    pending = None
    for j in range(N_FF_CHUNKS):
        if j + FF_AHEAD < N_FF_CHUNKS:
            up_project(j + FF_AHEAD)
        if pending is not None:
            if pending == 0:
                s.acc[0:rows] = down_project(pending)
            else:
                s.acc[0:rows] += down_project(pending)
            pending = None
        gate = conv_rows(j, 0)
        val = conv_rows(j, 1)
        grp = group_of[j]
        k = j - starts[grp]
        s.a[grp % 2, 0:rows, k * FF_CHUNK:(k + 1) * FF_CHUNK] = (gate * _sigmoid(gate) * val).astype(bf16)
        if j + 1 == starts[grp + 1] and grp < len(FF_GROUPS) - 1:
            pending = grp
    return _layer_norm(ALPHA * x + s.acc[0:rows] + down_project(len(FF_GROUPS) - 1), w.lng[...], w.lnb[...])


def _ffn_kernel(x_ref, meta_ref, wup, wdn, wcv, bcv, lng, lnb, out_ref, mout_ref,
                xb, u, ucarry, a, acc):
    w = _Refs(wup=wup, wcv=wcv, bcv=bcv, wdn=wdn, lng=lng, lnb=lnb)
    s = _Refs(xb=xb, u=u, ucarry=ucarry, a=a, acc=acc)

    @pl.when(pl.program_id(1) == 0)
    def _():
        ucarry[...] = jnp.zeros(ucarry.shape, jnp.float32)
        mout_ref[0] = _ffn_rows(N_META, meta_ref[0], w, s)

    out_ref[0] = _ffn_rows(FFN_ROW_TILE, x_ref[0], w, s)


def _weight_spec(a, layer):
    if a.ndim == 3:
        return pl.BlockSpec((None,) + a.shape[1:], lambda b, i: (layer, 0, 0), pipeline_mode=pl.Buffered(1))
    zeros = (0,) * a.ndim
    return pl.BlockSpec(a.shape, lambda b, i: zeros, pipeline_mode=pl.Buffered(1))


def _row_spec(tm):
    return pl.BlockSpec((1, tm, D_MODEL), lambda b, i: (b, i, 0))


def _next_row_spec(batch, tiles_per_seq):
    last = batch * tiles_per_seq - 1

    def index(b, i):
        nxt = jnp.minimum(b * tiles_per_seq + i + 1, last)
        return (nxt // tiles_per_seq, nxt % tiles_per_seq, 0)

    return pl.BlockSpec((1, ROW_TILE, D_MODEL), index)


def _meta_spec():
    return pl.BlockSpec((1, N_META, D_MODEL), lambda b, i: (b, 0, 0))


def _compiler_params():
    return pltpu.CompilerParams(dimension_semantics=("arbitrary", "arbitrary"),
                                vmem_limit_bytes=VMEM_LIMIT_BYTES)


def _out_shapes(x, meta):
    return (jax.ShapeDtypeStruct(x.shape, jnp.float32), jax.ShapeDtypeStruct(meta.shape, jnp.float32))


def _mixer_call(layer, x, meta, *weights):
    batch, rows, _ = x.shape
    tm = ROW_TILE
    nc = tm // CHUNK
    f32 = jnp.float32
    return pl.pallas_call(
        functools.partial(_mixer_kernel, layer),
        grid=(batch, rows // tm),
        in_specs=[_row_spec(tm), _next_row_spec(batch, rows // tm), _meta_spec()]
        + [_weight_spec(a, layer) for a in weights],
        out_specs=(_row_spec(tm), _meta_spec()),
        out_shape=_out_shapes(x, meta),
        scratch_shapes=[
            pltpu.VMEM((2 * HALF, CONV_WIDTH), f32),
            pltpu.VMEM((POOL_HIST, POOL_WIDTH), f32),
            pltpu.VMEM((HG_HEADS, HG_HEAD_DIM, HG_HEAD_DIM), f32),
            pltpu.VMEM((nc, CHUNK, HG_WIDTH), f32),
            pltpu.VMEM((nc, CHUNK, HG_WIDTH), f32),
            pltpu.VMEM((nc, CHUNK, HG_WIDTH), f32),
            pltpu.VMEM((nc, CHUNK, HG_WIDTH), f32),
            pltpu.VMEM((nc, CHUNK, HG_WIDTH), f32),
            pltpu.VMEM((tm, HG_WIDTH), f32),
            pltpu.VMEM((tm, D_MODEL), jnp.bfloat16),
            pltpu.VMEM((tm, D_MODEL), f32),
            pltpu.VMEM((tm, D_IN), f32),
            pltpu.VMEM((tm, D_MODEL), jnp.bfloat16),
        ],
        compiler_params=_compiler_params(),
        name=f"mixer_l{layer}",
    )(x, x, meta, *weights)


def _ffn_call(layer, x, meta, *weights):
    batch, rows, _ = x.shape
    tm = FFN_ROW_TILE
    f32 = jnp.float32
    return pl.pallas_call(
        _ffn_kernel,
        grid=(batch, rows // tm),
        in_specs=[_row_spec(tm), _meta_spec()] + [_weight_spec(a, layer) for a in weights],
        out_specs=(_row_spec(tm), _meta_spec()),
        out_shape=_out_shapes(x, meta),
        scratch_shapes=[
            pltpu.VMEM((tm, D_MODEL), jnp.bfloat16),
            pltpu.VMEM((FF_AHEAD_SLOTS, tm, 2 * FF_CHUNK), f32),
            pltpu.VMEM((N_FF_CHUNKS, HALF, 2 * FF_CHUNK), f32),
            pltpu.VMEM((2, tm, max(FF_GROUPS) * FF_CHUNK), jnp.bfloat16),
            pltpu.VMEM((tm, D_MODEL), f32),
        ],
        compiler_params=_compiler_params(),
        name=f"ffn_l{layer}",
    )(x, meta, *weights)


def kernel(x, meta_tokens, hg_lower_bounds, w_in, w_conv, w_pool, pool_scale, hg_norm_g, w_o, ln1_g, ln1_b,
           w_up, w_ffn_conv, b_ffn_conv, w_down, ln2_g, ln2_b):
    bf16 = jnp.bfloat16
    batch, seq, _ = x.shape
    assert seq % ROW_TILE == 0 and ROW_TILE % (CHUNK * CHUNK_UNROLL) == 0 and D_IN % PROJ_PIECE == 0
    assert seq % FFN_ROW_TILE == 0 and sum(FF_GROUPS) == N_FF_CHUNKS
    assert POOL_HIST <= N_META <= CHUNK and N_META % (2 * HALF) == 0 and POOL_HIST % HALF == 0
    h = x
    meta = jnp.broadcast_to(meta_tokens[None].astype(x.dtype), (batch, N_META, D_MODEL))
    hb = hg_lower_bounds.astype(jnp.float32)
    w_in_b, w_o_b, w_up_b, w_down_b = (a.astype(bf16) for a in (w_in, w_o, w_up, w_down))
    for l in range(DEPTH):
        wpool_bd = jax.scipy.linalg.block_diag(*[w_pool[l, g] for g in range(len(POOL_WINDOWS))])
        h, meta = _mixer_call(l, h, meta, w_in_b, wpool_bd.astype(bf16), w_o_b, hb, w_conv[l].T,
                              pool_scale[l][None], hg_norm_g[l][None], ln1_g[l][None], ln1_b[l][None])
        h, meta = _ffn_call(l, h, meta, w_up_b, w_down_b, w_ffn_conv[l].T, b_ffn_conv[l][None],
                            ln2_g[l][None], ln2_b[l][None])
    return h
```

```python
import functools

import jax
import jax.numpy as jnp
from jax import lax
from jax.experimental import pallas as pl
from jax.experimental.pallas import tpu as pltpu

D_MODEL = 1024
N_META = 16
CONV_WIDTH = 256
HG_WIDTH = 512
HG_HEAD_DIM = 128
HG_HEADS = 4
POOL_WIDTH = 256
POOL_WINDOWS = (2, 4, 8, 16)
POOL_GROUP_DIM = 64
D_FF = 2816
D_IN = 3 * CONV_WIDTH + 4 * HG_WIDTH + POOL_WIDTH
DEPTH = 2
ALPHA = (2 * DEPTH) ** 0.25
LN_EPS = 1e-5
RMS_EPS = 1e-6
F_FLOOR = 1e-30
LOG2_E = 1.4426950408889634

ROW_TILE = 512
FFN_ROW_TILE = 512
CHUNK = 32
HALF = 8
POOL_HIST = max(POOL_WINDOWS)
CHUNK_UNROLL = 4
PROJ_PIECE = 256
FF_CHUNK = 256
N_FF_CHUNKS = D_FF // FF_CHUNK
FF_GROUPS = (6, 3, 2)
FF_AHEAD = 2
FF_AHEAD_SLOTS = FF_AHEAD + 1
VMEM_LIMIT_BYTES = 56 * 1024 * 1024

_C_CONV = 0
_C_HG = 3 * CONV_WIDTH
_C_POOL = _C_HG + 4 * HG_WIDTH


def _layer_norm(r, g, b):
    mu = jnp.mean(r, axis=-1, keepdims=True)
    c = r - mu
    var = jnp.mean(c * c, axis=-1, keepdims=True)
    return c * lax.rsqrt(var + LN_EPS) * g + b


def _sigmoid(x):
    return 1.0 / (1.0 + jnp.exp(-x))


def _shifted_rows(u, prev8, k):
    hist = jnp.concatenate([prev8, u[0:HALF]], axis=0)
    head = pltpu.roll(hist, k, 0)[HALF:2 * HALF]
    if u.shape[0] == HALF:
        return head
    return jnp.concatenate([head, pltpu.roll(u, k, 0)[HALF:]], axis=0)


def _mixer_rows(layer, rows, is_meta, x, proj, between, w, s):
    f32, bf16 = jnp.float32, jnp.bfloat16
    cl = min(CHUNK, rows)
    nc = rows // cl
    tiles = cl // HALF
    between = list(between)

    hh = proj(_C_HG, _C_POOL)
    hq = hh[:, 0:HG_WIDTH]
    fz = hh[:, HG_WIDTH:2 * HG_WIDTH]
    hi = hh[:, 2 * HG_WIDTH:3 * HG_WIDTH]
    gz = hh[:, 3 * HG_WIDTH:4 * HG_WIDTH]

    hb = w.hb[...]
    e = jnp.exp(hb - jnp.max(hb, axis=0, keepdims=True))
    p = e / jnp.sum(e, axis=0, keepdims=True)
    lb = jnp.sum(p[0:layer + 1], axis=0, keepdims=True) - p[0:1]

    sig = _sigmoid(fz)
    f = lb + (1.0 - lb) * sig
    g = (jnp.log(jnp.maximum(f, F_FLOOR)) * LOG2_E).reshape(rows // HALF, HALF, HG_WIDTH)
    rid3 = lax.broadcasted_iota(jnp.int32, g.shape, 1)
    sh = 1
    while sh < HALF:
        g = g + jnp.where(rid3 >= sh, pltpu.roll(g, sh, 1), 0.0)
        sh *= 2
    g = g.reshape(nc, tiles, HALF, HG_WIDTH)
    run = g[:, 0]
    s.g[0:nc, 0:HALF, :] = run
    for j in range(1, tiles):
        run = g[:, j] + run[:, HALF - 1:HALF, :]
        s.g[0:nc, j * HALF:(j + 1) * HALF, :] = run
    s.q[0:nc, 0:cl] = (hq * (HG_HEAD_DIM ** -0.5)).reshape(nc, cl, HG_WIDTH)
    s.k[0:nc, 0:cl] = ((1.0 - lb) * (1.0 - sig)).reshape(nc, cl, HG_WIDTH)
    s.i[0:nc, 0:cl] = hi.reshape(nc, cl, HG_WIDTH)
    s.gate[0:rows] = gz * _sigmoid(gz)

    hc = proj(_C_CONV, _C_HG)
    cb = hc[:, 0:CONV_WIDTH]
    z = hc[:, CONV_WIDTH:2 * CONV_WIDTH] * hc[:, 2 * CONV_WIDTH:3 * CONV_WIDTH]
    zprev = s.zcarry[0:HALF]
    s.zcarry[0:HALF] = z[rows - HALF:rows]
    wc = w.wconv[...]
    conv = wc[0:1] * _shifted_rows(z, zprev, 2) + wc[1:2] * _shifted_rows(z, zprev, 1) + wc[2:3] * z
    s.ymix[0:rows, 0:CONV_WIDTH] = (cb * conv).astype(bf16)

    pv = proj(_C_POOL, D_IN)
    ext = jnp.concatenate([s.vcarry[...], pv], axis=0)
    s.vcarry[...] = pv[rows - POOL_HIST:rows]
    sums = {1: ext}
    width = 1
    while width < POOL_HIST:
        sums[2 * width] = sums[width] + pltpu.roll(sums[width], width, 0)
        width *= 2
    lane = lax.broadcasted_iota(jnp.int32, (1, POOL_WIDTH), 1)
    win = jnp.full((1, POOL_WIDTH), POOL_WINDOWS[-1], f32)
    wsum = sums[POOL_WINDOWS[-1]][POOL_HIST:]
    for gi in reversed(range(len(POOL_WINDOWS) - 1)):
        in_lower_groups = lane < (gi + 1) * POOL_GROUP_DIM
        win = jnp.where(in_lower_groups, float(POOL_WINDOWS[gi]), win)
        wsum = jnp.where(in_lower_groups, sums[POOL_WINDOWS[gi]][POOL_HIST:], wsum)
    if is_meta:
        pos = lax.broadcasted_iota(jnp.int32, (rows, 1), 0).astype(f32)
        count = jnp.minimum(pos + 1.0, win)
    else:
        count = win
    dpool = wsum / count - pv
    ypool = jnp.dot(dpool.astype(bf16), w.wpool[...], preferred_element_type=f32) * w.pscale[...]
    s.ymix[0:rows, CONV_WIDTH + HG_WIDTH:D_MODEL] = ypool.astype(bf16)

    s.mix[0:rows] = (
        jnp.dot(s.ymix[0:rows, 0:CONV_WIDTH], w.wo[0:CONV_WIDTH, :], preferred_element_type=f32)
        + jnp.dot(s.ymix[0:rows, CONV_WIDTH + HG_WIDTH:D_MODEL], w.wo[CONV_WIDTH + HG_WIDTH:D_MODEL, :],
                  preferred_element_type=f32))

    rid = lax.broadcasted_iota(jnp.int32, (HALF, 1), 0)
    sgn4 = jnp.where(rid < 4, -1.0, 1.0).astype(f32)
    sgn2 = jnp.where((rid & 2) == 0, -1.0, 1.0).astype(f32)
    odd = (rid & 1) == 1
    lo4 = rid < 4
    nst = HG_HEADS * cl
    tt = lax.broadcasted_iota(jnp.int32, (nst, nst), 0)
    uu = lax.broadcasted_iota(jnp.int32, (nst, nst), 1)
    levels = [1 << b for b in range(cl.bit_length() - 1)]
    masks = [((tt // (2 * m)) == (uu // (2 * m))) & ((tt % (2 * m)) >= m) & ((uu % (2 * m)) < m)
             for m in levels]
    diag = tt == uu
    contract_lanes = (((1,), (1,)), ((), ()))
    head_lanes = [slice(h * HG_HEAD_DIM, (h + 1) * HG_HEAD_DIM) for h in range(HG_HEADS)]

    def stack_heads(v):
        return jnp.concatenate([v[:, ls] for ls in head_lanes], axis=0)

    def level_exponents(c, g_tiles, m):
        out = []
        for j, gt in enumerate(g_tiles):
            r0 = j * HALF
            if m >= HALF:
                ref = (r0 // (2 * m)) * 2 * m + m - 1
                gref = s.g[c, ref:ref + 1, :]
                out.append(gref - gt if (r0 % (2 * m)) < m else gt - gref)
            elif m == 4:
                out.append(sgn4 * (gt - s.g[c, r0 + 3:r0 + 4, :]))
            elif m == 2:
                out.append(sgn2 * (gt - jnp.where(lo4, s.g[c, r0 + 1:r0 + 2, :], s.g[c, r0 + 5:r0 + 6, :])))
            else:
                out.append(jnp.where(odd, gt - pltpu.roll(gt, 1, 0), 0.0))
        return jnp.concatenate(out, axis=0)

    def emit_between():
        if between:
            between.pop(0)()

    def process_chunks(chunks):
        gcs = [s.g[c, 0:cl, :] for c in chunks]
        ibs = [s.i[c, 0:cl, :].astype(bf16) for c in chunks]
        states = [[s.st[h]] for h in range(HG_HEADS)]
        for n, c in enumerate(chunks):
            gl = gcs[n][cl - 1:cl]
            kt = (s.k[c, 0:cl, :] * jnp.exp2(gl - gcs[n])).astype(bf16)
            decay = jnp.exp2(gl)
            for h, ls in enumerate(head_lanes):
                delta = lax.dot_general(ibs[n][:, ls], kt[:, ls], (((0,), (0,)), ((), ())),
                                        preferred_element_type=f32)
                states[h].append(decay[:, ls] * states[h][-1] + delta)
        for h in range(HG_HEADS):
            s.st[h] = states[h][-1]
        emit_between()
        amats = []
        for n, c in enumerate(chunks):
            g_tiles = [gcs[n][j * HALF:(j + 1) * HALF] for j in range(tiles)]
            qb = s.q[c, 0:cl, :].astype(bf16)
            kf = s.k[c, 0:cl, :]
            a = jnp.where(diag, jnp.sum(stack_heads(s.q[c, 0:cl, :] * kf), axis=-1, keepdims=True), 0.0)
            for m, mask in zip(levels, masks):
                ef = jnp.exp2(level_exponents(c, g_tiles, m))
                a = jnp.where(mask, jnp.dot(stack_heads(qb * ef.astype(bf16)), stack_heads(kf * ef).T.astype(bf16),
                                            preferred_element_type=f32), a)
            amats.append(a.astype(bf16))
        emit_between()
        inters = []
        for n, c in enumerate(chunks):
            qt = (s.q[c, 0:cl, :] * jnp.exp2(gcs[n])).astype(bf16)
            inters.append([jnp.dot(qt[:, ls], states[h][n].T.astype(bf16), preferred_element_type=f32)
                           for h, ls in enumerate(head_lanes)])
        emit_between()
        for n, c in enumerate(chunks):
            intra = jnp.dot(amats[n], stack_heads(ibs[n]), preferred_element_type=f32)
            for h, ls in enumerate(head_lanes):
                s.o[c, 0:cl, ls] = inters[n][h] + intra[h * cl:(h + 1) * cl]

    for c0 in range(0, nc, CHUNK_UNROLL):
        process_chunks(list(range(c0, min(c0 + CHUNK_UNROLL, nc))))
    while between:
        emit_between()

    gn = w.gnorm[...]
    for h in range(HG_HEADS):
        ls = slice(h * HG_HEAD_DIM, (h + 1) * HG_HEAD_DIM)
        o = s.o[0:nc, 0:cl, ls].reshape(rows, HG_HEAD_DIM)
        o = o * lax.rsqrt(jnp.mean(o * o, axis=-1, keepdims=True) + RMS_EPS) * gn
        o = o * s.gate[0:rows, ls]
        s.ymix[0:rows, CONV_WIDTH + h * HG_HEAD_DIM:CONV_WIDTH + (h + 1) * HG_HEAD_DIM] = o.astype(bf16)

    mix = s.mix[0:rows] + jnp.dot(s.ymix[0:rows, CONV_WIDTH:CONV_WIDTH + HG_WIDTH],
                                  w.wo[CONV_WIDTH:CONV_WIDTH + HG_WIDTH, :], preferred_element_type=f32)
    return _layer_norm(ALPHA * x + mix, w.lng[...], w.lnb[...])


class _Refs:
    def __init__(self, **kw):
        self.__dict__.update(kw)


def _mixer_kernel(layer, x_ref, xnext_ref, meta_ref, win, wpool, wo, hb, wconv, pscale, gnorm, lng, lnb,
                  out_ref, mout_ref,
                  zcarry, vcarry, st, q, k, i, g, o, gate, ymix, mix, hproj, xnb):
    f32, bf16 = jnp.float32, jnp.bfloat16
    w = _Refs(hb=hb, win=win, wconv=wconv, wpool=wpool, pscale=pscale, gnorm=gnorm, wo=wo, lng=lng, lnb=lnb)
    s = _Refs(zcarry=zcarry, vcarry=vcarry, st=st, q=q, k=k, i=i, g=g, o=o, gate=gate, ymix=ymix, mix=mix)
    first_of_batch = pl.program_id(1) == 0

    @pl.when(first_of_batch & (pl.program_id(0) == 0))
    def _():
        hproj[...] = jnp.dot(x_ref[0].astype(bf16), win[...], preferred_element_type=f32)

    @pl.when(first_of_batch)
    def _():
        zcarry[...] = jnp.zeros(zcarry.shape, f32)
        vcarry[...] = jnp.zeros(vcarry.shape, f32)
        st[...] = jnp.zeros(st.shape, f32)
        mb = meta_ref[0].astype(bf16)
        mout_ref[0] = _mixer_rows(
            layer, N_META, True, meta_ref[0],
            lambda c0, c1: jnp.dot(mb, win[:, c0:c1], preferred_element_type=f32), [], w, s)

    xnb[...] = xnext_ref[0].astype(bf16)

    def piece(c0):
        def run():
            hproj[:, c0:c0 + PROJ_PIECE] = jnp.dot(xnb[...], win[:, c0:c0 + PROJ_PIECE], preferred_element_type=f32)
        return run

    out_ref[0] = _mixer_rows(layer, ROW_TILE, False, x_ref[0], lambda c0, c1: hproj[:, c0:c1],
                             [piece(c0) for c0 in range(0, D_IN, PROJ_PIECE)], w, s)


def _ffn_rows(rows, x, w, s):
    f32, bf16 = jnp.float32, jnp.bfloat16
    s.xb[0:rows] = x.astype(bf16)

    def up_project(j):
        for part in range(2):
            c0 = part * D_FF + j * FF_CHUNK
            s.u[j % FF_AHEAD_SLOTS, 0:rows, part * FF_CHUNK:(part + 1) * FF_CHUNK] = jnp.dot(
                s.xb[0:rows], w.wup[:, c0:c0 + FF_CHUNK], preferred_element_type=f32)

    def conv_rows(j, part):
        c0 = part * D_FF + j * FF_CHUNK
        u = s.u[j % FF_AHEAD_SLOTS, 0:rows, part * FF_CHUNK:(part + 1) * FF_CHUNK]
        prev8 = s.ucarry[j, :, part * FF_CHUNK:(part + 1) * FF_CHUNK]
        s.ucarry[j, :, part * FF_CHUNK:(part + 1) * FF_CHUNK] = u[rows - HALF:rows]
        wc = w.wcv[:, c0:c0 + FF_CHUNK]
        return (wc[0:1] * _shifted_rows(u, prev8, 2) + wc[1:2] * _shifted_rows(u, prev8, 1) + wc[2:3] * u
                + w.bcv[:, c0:c0 + FF_CHUNK])

    starts = [sum(FF_GROUPS[:g]) for g in range(len(FF_GROUPS) + 1)]
    group_of = {j: g for g in range(len(FF_GROUPS)) for j in range(starts[g], starts[g + 1])}

    def down_project(grp):
        j0, j1 = starts[grp], starts[grp + 1]
        return jnp.dot(s.a[grp % 2, 0:rows, 0:(j1 - j0) * FF_CHUNK], w.wdn[j0 * FF_CHUNK:j1 * FF_CHUNK, :],
                       preferred_element_type=f32)

    for j in range(FF_AHEAD):
        up_project(j)
    pending = None
    for j in range(N_FF_CHUNKS):
        if j + FF_AHEAD < N_FF_CHUNKS:
            up_project(j + FF_AHEAD)
        if pending is not None:
            if pending == 0:
                s.acc[0:rows] = down_project(pending)
            else:
                s.acc[0:rows] += down_project(pending)
            pending = None
        gate = conv_rows(j, 0)
        val = conv_rows(j, 1)
        grp = group_of[j]
        k = j - starts[grp]
        s.a[grp % 2, 0:rows, k * FF_CHUNK:(k + 1) * FF_CHUNK] = (gate * _sigmoid(gate) * val).astype(bf16)
        if j + 1 == starts[grp + 1] and grp < len(FF_GROUPS) - 1:
            pending = grp
    return _layer_norm(ALPHA * x + s.acc[0:rows] + down_project(len(FF_GROUPS) - 1), w.lng[...], w.lnb[...])


def _ffn_kernel(x_ref, meta_ref, wup, wdn, wcv, bcv, lng, lnb, out_ref, mout_ref,
                xb, u, ucarry, a, acc):
    w = _Refs(wup=wup, wcv=wcv, bcv=bcv, wdn=wdn, lng=lng, lnb=lnb)
    s = _Refs(xb=xb, u=u, ucarry=ucarry, a=a, acc=acc)

    @pl.when(pl.program_id(1) == 0)
    def _():
        ucarry[...] = jnp.zeros(ucarry.shape, jnp.float32)
        mout_ref[0] = _ffn_rows(N_META, meta_ref[0], w, s)

    out_ref[0] = _ffn_rows(FFN_ROW_TILE, x_ref[0], w, s)


def _weight_spec(a, layer):
    if a.ndim == 3:
        return pl.BlockSpec((None,) + a.shape[1:], lambda b, i: (layer, 0, 0), pipeline_mode=pl.Buffered(1))
    zeros = (0,) * a.ndim
    return pl.BlockSpec(a.shape, lambda b, i: zeros, pipeline_mode=pl.Buffered(1))


def _row_spec(tm):
    return pl.BlockSpec((1, tm, D_MODEL), lambda b, i: (b, i, 0))


def _next_row_spec(batch, tiles_per_seq):
    last = batch * tiles_per_seq - 1

    def index(b, i):
        nxt = jnp.minimum(b * tiles_per_seq + i + 1, last)
        return (nxt // tiles_per_seq, nxt % tiles_per_seq, 0)

    return pl.BlockSpec((1, ROW_TILE, D_MODEL), index)


def _meta_spec():
    return pl.BlockSpec((1, N_META, D_MODEL), lambda b, i: (b, 0, 0))


def _compiler_params():
    return pltpu.CompilerParams(dimension_semantics=("arbitrary", "arbitrary"),
                                vmem_limit_bytes=VMEM_LIMIT_BYTES)


def _out_shapes(x, meta):
    return (jax.ShapeDtypeStruct(x.shape, jnp.float32), jax.ShapeDtypeStruct(meta.shape, jnp.float32))


def _mixer_call(layer, x, meta, *weights):
    batch, rows, _ = x.shape
    tm = ROW_TILE
    nc = tm // CHUNK
    f32 = jnp.float32
    return pl.pallas_call(
        functools.partial(_mixer_kernel, layer),
        grid=(batch, rows // tm),
        in_specs=[_row_spec(tm), _next_row_spec(batch, rows // tm), _meta_spec()]
        + [_weight_spec(a, layer) for a in weights],
        out_specs=(_row_spec(tm), _meta_spec()),
        out_shape=_out_shapes(x, meta),
        scratch_shapes=[
            pltpu.VMEM((2 * HALF, CONV_WIDTH), f32),
            pltpu.VMEM((POOL_HIST, POOL_WIDTH), f32),
            pltpu.VMEM((HG_HEADS, HG_HEAD_DIM, HG_HEAD_DIM), f32),
            pltpu.VMEM((nc, CHUNK, HG_WIDTH), f32),
            pltpu.VMEM((nc, CHUNK, HG_WIDTH), f32),
            pltpu.VMEM((nc, CHUNK, HG_WIDTH), f32),
            pltpu.VMEM((nc, CHUNK, HG_WIDTH), f32),
            pltpu.VMEM((nc, CHUNK, HG_WIDTH), f32),
            pltpu.VMEM((tm, HG_WIDTH), f32),
            pltpu.VMEM((tm, D_MODEL), jnp.bfloat16),
            pltpu.VMEM((tm, D_MODEL), f32),
            pltpu.VMEM((tm, D_IN), f32),
            pltpu.VMEM((tm, D_MODEL), jnp.bfloat16),
        ],
        compiler_params=_compiler_params(),
        name=f"mixer_l{layer}",
    )(x, x, meta, *weights)


def _ffn_call(layer, x, meta, *weights):
    batch, rows, _ = x.shape
    tm = FFN_ROW_TILE
    f32 = jnp.float32
    return pl.pallas_call(
        _ffn_kernel,
        grid=(batch, rows // tm),
        in_specs=[_row_spec(tm), _meta_spec()] + [_weight_spec(a, layer) for a in weights],
        out_specs=(_row_spec(tm), _meta_spec()),
        out_shape=_out_shapes(x, meta),
        scratch_shapes=[
            pltpu.VMEM((tm, D_MODEL), jnp.bfloat16),
            pltpu.VMEM((FF_AHEAD_SLOTS, tm, 2 * FF_CHUNK), f32),
            pltpu.VMEM((N_FF_CHUNKS, HALF, 2 * FF_CHUNK), f32),
            pltpu.VMEM((2, tm, max(FF_GROUPS) * FF_CHUNK), jnp.bfloat16),
            pltpu.VMEM((tm, D_MODEL), f32),
        ],
        compiler_params=_compiler_params(),
        name=f"ffn_l{layer}",
    )(x, meta, *weights)


def kernel(x, meta_tokens, hg_lower_bounds, w_in, w_conv, w_pool, pool_scale, hg_norm_g, w_o, ln1_g, ln1_b,
           w_up, w_ffn_conv, b_ffn_conv, w_down, ln2_g, ln2_b):
    bf16 = jnp.bfloat16
    batch, seq, _ = x.shape
    assert seq % ROW_TILE == 0 and ROW_TILE % (CHUNK * CHUNK_UNROLL) == 0 and D_IN % PROJ_PIECE == 0
    assert seq % FFN_ROW_TILE == 0 and sum(FF_GROUPS) == N_FF_CHUNKS
    assert POOL_HIST <= N_META <= CHUNK and N_META % (2 * HALF) == 0 and POOL_HIST % HALF == 0
    h = x
    meta = jnp.broadcast_to(meta_tokens[None].astype(x.dtype), (batch, N_META, D_MODEL))
    hb = hg_lower_bounds.astype(jnp.float32)
    w_in_b, w_o_b, w_up_b, w_down_b = (a.astype(bf16) for a in (w_in, w_o, w_up, w_down))
    for l in range(DEPTH):
        wpool_bd = jax.scipy.linalg.block_diag(*[w_pool[l, g] for g in range(len(POOL_WINDOWS))])
        h, meta = _mixer_call(l, h, meta, w_in_b, wpool_bd.astype(bf16), w_o_b, hb, w_conv[l].T,
                              pool_scale[l][None], hg_norm_g[l][None], ln1_g[l][None], ln1_b[l][None])
        h, meta = _ffn_call(l, h, meta, w_up_b, w_down_b, w_ffn_conv[l].T, b_ffn_conv[l][None],
                            ln2_g[l][None], ln2_b[l][None])
    return h
```

```python
import functools

import jax
import jax.numpy as jnp
from jax import lax
from jax.experimental import pallas as pl
from jax.experimental.pallas import tpu as pltpu

D_MODEL = 1024
N_META = 16
CONV_WIDTH = 256
HG_WIDTH = 512
HG_HEAD_DIM = 128
HG_HEADS = 4
POOL_WIDTH = 256
POOL_WINDOWS = (2, 4, 8, 16)
POOL_GROUP_DIM = 64
D_FF = 2816
D_IN = 3 * CONV_WIDTH + 4 * HG_WIDTH + POOL_WIDTH
DEPTH = 2
ALPHA = (2 * DEPTH) ** 0.25
LN_EPS = 1e-5
RMS_EPS = 1e-6
F_FLOOR = 1e-30
LOG2_E = 1.4426950408889634

ROW_TILE = 512
FFN_ROW_TILE = 512
CHUNK = 32
HALF = 8
POOL_HIST = max(POOL_WINDOWS)
CHUNK_UNROLL = 4
PROJ_PIECE = 256
FF_CHUNK = 256
N_FF_CHUNKS = D_FF // FF_CHUNK
FF_GROUPS = (6, 3, 2)
FF_AHEAD = 2
FF_AHEAD_SLOTS = FF_AHEAD + 1
VMEM_LIMIT_BYTES = 56 * 1024 * 1024

_C_CONV = 0
_C_HG = 3 * CONV_WIDTH
_C_POOL = _C_HG + 4 * HG_WIDTH


def _layer_norm(r, g, b):
    mu = jnp.mean(r, axis=-1, keepdims=True)
    c = r - mu
    var = jnp.mean(c * c, axis=-1, keepdims=True)
    return c * lax.rsqrt(var + LN_EPS) * g + b


def _sigmoid(x):
    return 1.0 / (1.0 + jnp.exp(-x))


def _shifted_rows(u, prev8, k):
    hist = jnp.concatenate([prev8, u[0:HALF]], axis=0)
    head = pltpu.roll(hist, k, 0)[HALF:2 * HALF]
    if u.shape[0] == HALF:
        return head
    return jnp.concatenate([head, pltpu.roll(u, k, 0)[HALF:]], axis=0)


def _mixer_rows(layer, rows, is_meta, x, proj, between, w, s):
    f32, bf16 = jnp.float32, jnp.bfloat16
    cl = min(CHUNK, rows)
    nc = rows // cl
    tiles = cl // HALF
    between = list(between)

    hh = proj(_C_HG, _C_POOL)
    hq = hh[:, 0:HG_WIDTH]
    fz = hh[:, HG_WIDTH:2 * HG_WIDTH]
    hi = hh[:, 2 * HG_WIDTH:3 * HG_WIDTH]
    gz = hh[:, 3 * HG_WIDTH:4 * HG_WIDTH]

    hb = w.hb[...]
    e = jnp.exp(hb - jnp.max(hb, axis=0, keepdims=True))
    p = e / jnp.sum(e, axis=0, keepdims=True)
    lb = jnp.sum(p[0:layer + 1], axis=0, keepdims=True) - p[0:1]

    sig = _sigmoid(fz)
    f = lb + (1.0 - lb) * sig
    g = (jnp.log(jnp.maximum(f, F_FLOOR)) * LOG2_E).reshape(rows // HALF, HALF, HG_WIDTH)
    rid3 = lax.broadcasted_iota(jnp.int32, g.shape, 1)
    sh = 1
    while sh < HALF:
        g = g + jnp.where(rid3 >= sh, pltpu.roll(g, sh, 1), 0.0)
        sh *= 2
    g = g.reshape(nc, tiles, HALF, HG_WIDTH)
    run = g[:, 0]
    s.g[0:nc, 0:HALF, :] = run
    for j in range(1, tiles):
        run = g[:, j] + run[:, HALF - 1:HALF, :]
        s.g[0:nc, j * HALF:(j + 1) * HALF, :] = run
    s.q[0:nc, 0:cl] = (hq * (HG_HEAD_DIM ** -0.5)).reshape(nc, cl, HG_WIDTH)
    s.k[0:nc, 0:cl] = ((1.0 - lb) * (1.0 - sig)).reshape(nc, cl, HG_WIDTH)
    s.i[0:nc, 0:cl] = hi.reshape(nc, cl, HG_WIDTH)
    s.gate[0:rows] = gz * _sigmoid(gz)

    hc = proj(_C_CONV, _C_HG)
    cb = hc[:, 0:CONV_WIDTH]
    z = hc[:, CONV_WIDTH:2 * CONV_WIDTH] * hc[:, 2 * CONV_WIDTH:3 * CONV_WIDTH]
    zprev = s.zcarry[0:HALF]
    s.zcarry[0:HALF] = z[rows - HALF:rows]
    wc = w.wconv[...]
    conv = wc[0:1] * _shifted_rows(z, zprev, 2) + wc[1:2] * _shifted_rows(z, zprev, 1) + wc[2:3] * z
    s.ymix[0:rows, 0:CONV_WIDTH] = (cb * conv).astype(bf16)

    pv = proj(_C_POOL, D_IN)
    ext = jnp.concatenate([s.vcarry[...], pv], axis=0)
    s.vcarry[...] = pv[rows - POOL_HIST:rows]
    sums = {1: ext}
    width = 1
    while width < POOL_HIST:
        sums[2 * width] = sums[width] + pltpu.roll(sums[width], width, 0)
        width *= 2
    lane = lax.broadcasted_iota(jnp.int32, (1, POOL_WIDTH), 1)
    win = jnp.full((1, POOL_WIDTH), POOL_WINDOWS[-1], f32)
    wsum = sums[POOL_WINDOWS[-1]][POOL_HIST:]
    for gi in reversed(range(len(POOL_WINDOWS) - 1)):
        in_lower_groups = lane < (gi + 1) * POOL_GROUP_DIM
        win = jnp.where(in_lower_groups, float(POOL_WINDOWS[gi]), win)
        wsum = jnp.where(in_lower_groups, sums[POOL_WINDOWS[gi]][POOL_HIST:], wsum)
    if is_meta:
        pos = lax.broadcasted_iota(jnp.int32, (rows, 1), 0).astype(f32)
        count = jnp.minimum(pos + 1.0, win)
    else:
        count = win
    dpool = wsum / count - pv
    ypool = jnp.dot(dpool.astype(bf16), w.wpool[...], preferred_element_type=f32) * w.pscale[...]
    s.ymix[0:rows, CONV_WIDTH + HG_WIDTH:D_MODEL] = ypool.astype(bf16)

    s.mix[0:rows] = (
        jnp.dot(s.ymix[0:rows, 0:CONV_WIDTH], w.wo[0:CONV_WIDTH, :], preferred_element_type=f32)
        + jnp.dot(s.ymix[0:rows, CONV_WIDTH + HG_WIDTH:D_MODEL], w.wo[CONV_WIDTH + HG_WIDTH:D_MODEL, :],
                  preferred_element_type=f32))

    rid = lax.broadcasted_iota(jnp.int32, (HALF, 1), 0)
    sgn4 = jnp.where(rid < 4, -1.0, 1.0).astype(f32)
    sgn2 = jnp.where((rid & 2) == 0, -1.0, 1.0).astype(f32)
    odd = (rid & 1) == 1
    lo4 = rid < 4
    nst = HG_HEADS * cl
    tt = lax.broadcasted_iota(jnp.int32, (nst, nst), 0)
    uu = lax.broadcasted_iota(jnp.int32, (nst, nst), 1)
    levels = [1 << b for b in range(cl.bit_length() - 1)]
    masks = [((tt // (2 * m)) == (uu // (2 * m))) & ((tt % (2 * m)) >= m) & ((uu % (2 * m)) < m)
             for m in levels]
    diag = tt == uu
    contract_lanes = (((1,), (1,)), ((), ()))
    head_lanes = [slice(h * HG_HEAD_DIM, (h + 1) * HG_HEAD_DIM) for h in range(HG_HEADS)]

    def stack_heads(v):
        return jnp.concatenate([v[:, ls] for ls in head_lanes], axis=0)

    def level_exponents(c, g_tiles, m):
        out = []
        for j, gt in enumerate(g_tiles):
            r0 = j * HALF
            if m >= HALF:
                ref = (r0 // (2 * m)) * 2 * m + m - 1
                gref = s.g[c, ref:ref + 1, :]
                out.append(gref - gt if (r0 % (2 * m)) < m else gt - gref)
            elif m == 4:
                out.append(sgn4 * (gt - s.g[c, r0 + 3:r0 + 4, :]))
            elif m == 2:
                out.append(sgn2 * (gt - jnp.where(lo4, s.g[c, r0 + 1:r0 + 2, :], s.g[c, r0 + 5:r0 + 6, :])))
            else:
                out.append(jnp.where(odd, gt - pltpu.roll(gt, 1, 0), 0.0))
        return jnp.concatenate(out, axis=0)

    def emit_between():
        if between:
            between.pop(0)()

    def process_chunks(chunks):
        gcs = [s.g[c, 0:cl, :] for c in chunks]
        ibs = [s.i[c, 0:cl, :].astype(bf16) for c in chunks]
        states = [[s.st[h]] for h in range(HG_HEADS)]
        for n, c in enumerate(chunks):
            gl = gcs[n][cl - 1:cl]
            kt = (s.k[c, 0:cl, :] * jnp.exp2(gl - gcs[n])).astype(bf16)
            decay = jnp.exp2(gl)
            for h, ls in enumerate(head_lanes):
                delta = lax.dot_general(ibs[n][:, ls], kt[:, ls], (((0,), (0,)), ((), ())),
                                        preferred_element_type=f32)
                states[h].append(decay[:, ls] * states[h][-1] + delta)
        for h in range(HG_HEADS):
            s.st[h] = states[h][-1]
        emit_between()
        amats = []
        for n, c in enumerate(chunks):
            g_tiles = [gcs[n][j * HALF:(j + 1) * HALF] for j in range(tiles)]
            qb = s.q[c, 0:cl, :].astype(bf16)
            kf = s.k[c, 0:cl, :]
            qf = s.q[c, 0:cl, :]
            a = jnp.where(diag, jnp.sum(stack_heads(qf * kf), axis=-1, keepdims=True), 0.0)
            e1 = jnp.exp2(level_exponents(c, g_tiles, 1))
            a = jnp.where(masks[0], jnp.sum(stack_heads(qf * e1 * pltpu.roll(kf, 1, 0)), axis=-1, keepdims=True), a)
            for m, mask in zip(levels[1:], masks[1:]):
                ef = jnp.exp2(level_exponents(c, g_tiles, m))
                a = jnp.where(mask, jnp.dot(stack_heads(qb * ef.astype(bf16)), stack_heads(kf * ef).T.astype(bf16),
                                            preferred_element_type=f32), a)
            amats.append(a.astype(bf16))
        emit_between()
        inters = []
        for n, c in enumerate(chunks):
            qt = (s.q[c, 0:cl, :] * jnp.exp2(gcs[n])).astype(bf16)
            inters.append([jnp.dot(qt[:, ls], states[h][n].T.astype(bf16), preferred_element_type=f32)
                           for h, ls in enumerate(head_lanes)])
        emit_between()
        for n, c in enumerate(chunks):
            intra = jnp.dot(amats[n], stack_heads(ibs[n]), preferred_element_type=f32)
            for h, ls in enumerate(head_lanes):
                s.o[c, 0:cl, ls] = inters[n][h] + intra[h * cl:(h + 1) * cl]

    for c0 in range(0, nc, CHUNK_UNROLL):
        process_chunks(list(range(c0, min(c0 + CHUNK_UNROLL, nc))))
    while between:
        emit_between()

    gn = w.gnorm[...]
    for h in range(HG_HEADS):
        ls = slice(h * HG_HEAD_DIM, (h + 1) * HG_HEAD_DIM)
        o = s.o[0:nc, 0:cl, ls].reshape(rows, HG_HEAD_DIM)
        o = o * lax.rsqrt(jnp.mean(o * o, axis=-1, keepdims=True) + RMS_EPS) * gn
        o = o * s.gate[0:rows, ls]
        s.ymix[0:rows, CONV_WIDTH + h * HG_HEAD_DIM:CONV_WIDTH + (h + 1) * HG_HEAD_DIM] = o.astype(bf16)

    mix = s.mix[0:rows] + jnp.dot(s.ymix[0:rows, CONV_WIDTH:CONV_WIDTH + HG_WIDTH],
                                  w.wo[CONV_WIDTH:CONV_WIDTH + HG_WIDTH, :], preferred_element_type=f32)
    return _layer_norm(ALPHA * x + mix, w.lng[...], w.lnb[...])


class _Refs:
    def __init__(self, **kw):
        self.__dict__.update(kw)


def _mixer_kernel(layer, x_ref, xnext_ref, meta_ref, win, wpool, wo, hb, wconv, pscale, gnorm, lng, lnb,
                  out_ref, mout_ref,
                  zcarry, vcarry, st, q, k, i, g, o, gate, ymix, mix, hproj, xnb):
    f32, bf16 = jnp.float32, jnp.bfloat16
    w = _Refs(hb=hb, win=win, wconv=wconv, wpool=wpool, pscale=pscale, gnorm=gnorm, wo=wo, lng=lng, lnb=lnb)
    s = _Refs(zcarry=zcarry, vcarry=vcarry, st=st, q=q, k=k, i=i, g=g, o=o, gate=gate, ymix=ymix, mix=mix)
    first_of_batch = pl.program_id(1) == 0

    @pl.when(first_of_batch & (pl.program_id(0) == 0))
    def _():
        hproj[...] = jnp.dot(x_ref[0].astype(bf16), win[...], preferred_element_type=f32)

    @pl.when(first_of_batch)
    def _():
        zcarry[...] = jnp.zeros(zcarry.shape, f32)
        vcarry[...] = jnp.zeros(vcarry.shape, f32)
        st[...] = jnp.zeros(st.shape, f32)
        mb = meta_ref[0].astype(bf16)
        mout_ref[0] = _mixer_rows(
            layer, N_META, True, meta_ref[0],
            lambda c0, c1: jnp.dot(mb, win[:, c0:c1], preferred_element_type=f32), [], w, s)

    xnb[...] = xnext_ref[0].astype(bf16)

    def piece(c0):
        def run():
            hproj[:, c0:c0 + PROJ_PIECE] = jnp.dot(xnb[...], win[:, c0:c0 + PROJ_PIECE], preferred_element_type=f32)
        return run

    out_ref[0] = _mixer_rows(layer, ROW_TILE, False, x_ref[0], lambda c0, c1: hproj[:, c0:c1],
                             [piece(c0) for c0 in range(0, D_IN, PROJ_PIECE)], w, s)


def _ffn_rows(rows, x, w, s):
    f32, bf16 = jnp.float32, jnp.bfloat16
    s.xb[0:rows] = x.astype(bf16)

    def up_project(j):
        for part in range(2):
            c0 = part * D_FF + j * FF_CHUNK
            s.u[j % FF_AHEAD_SLOTS, 0:rows, part * FF_CHUNK:(part + 1) * FF_CHUNK] = jnp.dot(
                s.xb[0:rows], w.wup[:, c0:c0 + FF_CHUNK], preferred_element_type=f32)

    def conv_rows(j, part):
        c0 = part * D_FF + j * FF_CHUNK
        u = s.u[j % FF_AHEAD_SLOTS, 0:rows, part * FF_CHUNK:(part + 1) * FF_CHUNK]
        prev8 = s.ucarry[j, :, part * FF_CHUNK:(part + 1) * FF_CHUNK]
        s.ucarry[j, :, part * FF_CHUNK:(part + 1) * FF_CHUNK] = u[rows - HALF:rows]
        wc = w.wcv[:, c0:c0 + FF_CHUNK]
        return (wc[0:1] * _shifted_rows(u, prev8, 2) + wc[1:2] * _shifted_rows(u, prev8, 1) + wc[2:3] * u
                + w.bcv[:, c0:c0 + FF_CHUNK])

    starts = [sum(FF_GROUPS[:g]) for g in range(len(FF_GROUPS) + 1)]
    group_of = {j: g for g in range(len(FF_GROUPS)) for j in range(starts[g], starts[g + 1])}

    def down_project(grp):
        j0, j1 = starts[grp], starts[grp + 1]
        return jnp.dot(s.a[grp % 2, 0:rows, 0:(j1 - j0) * FF_CHUNK], w.wdn[j0 * FF_CHUNK:j1 * FF_CHUNK, :],
                       preferred_element_type=f32)

    for j in range(FF_AHEAD):
        up_project(j)
    pending = None
    for j in range(N_FF_CHUNKS):
        if j + FF_AHEAD < N_FF_CHUNKS:
            up_project(j + FF_AHEAD)
        if pending is not None:
            if pending == 0:
                s.acc[0:rows] = down_project(pending)
            else:
                s.acc[0:rows] += down_project(pending)
            pending = None
        gate = conv_rows(j, 0)
        val = conv_rows(j, 1)
        grp = group_of[j]
        k = j - starts[grp]
        s.a[grp % 2, 0:rows, k * FF_CHUNK:(k + 1) * FF_CHUNK] = (gate * _sigmoid(gate) * val).astype(bf16)
        if j + 1 == starts[grp + 1] and grp < len(FF_GROUPS) - 1:
            pending = grp
    return _layer_norm(ALPHA * x + s.acc[0:rows] + down_project(len(FF_GROUPS) - 1), w.lng[...], w.lnb[...])


def _ffn_kernel(x_ref, meta_ref, wup, wdn, wcv, bcv, lng, lnb, out_ref, mout_ref,
                xb, u, ucarry, a, acc):
    w = _Refs(wup=wup, wcv=wcv, bcv=bcv, wdn=wdn, lng=lng, lnb=lnb)
    s = _Refs(xb=xb, u=u, ucarry=ucarry, a=a, acc=acc)

    @pl.when(pl.program_id(1) == 0)
    def _():
        ucarry[...] = jnp.zeros(ucarry.shape, jnp.float32)
        mout_ref[0] = _ffn_rows(N_META, meta_ref[0], w, s)

    out_ref[0] = _ffn_rows(FFN_ROW_TILE, x_ref[0], w, s)


def _weight_spec(a, layer):
    if a.ndim == 3:
        return pl.BlockSpec((None,) + a.shape[1:], lambda b, i: (layer, 0, 0), pipeline_mode=pl.Buffered(1))
    zeros = (0,) * a.ndim
    return pl.BlockSpec(a.shape, lambda b, i: zeros, pipeline_mode=pl.Buffered(1))


def _row_spec(tm):
    return pl.BlockSpec((1, tm, D_MODEL), lambda b, i: (b, i, 0))


def _next_row_spec(batch, tiles_per_seq):
    last = batch * tiles_per_seq - 1

    def index(b, i):
        nxt = jnp.minimum(b * tiles_per_seq + i + 1, last)
        return (nxt // tiles_per_seq, nxt % tiles_per_seq, 0)

    return pl.BlockSpec((1, ROW_TILE, D_MODEL), index)


def _meta_spec():
    return pl.BlockSpec((1, N_META, D_MODEL), lambda b, i: (b, 0, 0))


def _compiler_params():
    return pltpu.CompilerParams(dimension_semantics=("arbitrary", "arbitrary"),
                                vmem_limit_bytes=VMEM_LIMIT_BYTES)


def _out_shapes(x, meta):
    return (jax.ShapeDtypeStruct(x.shape, jnp.float32), jax.ShapeDtypeStruct(meta.shape, jnp.float32))


def _mixer_call(layer, x, meta, *weights):
    batch, rows, _ = x.shape
    tm = ROW_TILE
    nc = tm // CHUNK
    f32 = jnp.float32
    return pl.pallas_call(
        functools.partial(_mixer_kernel, layer),
        grid=(batch, rows // tm),
        in_specs=[_row_spec(tm), _next_row_spec(batch, rows // tm), _meta_spec()]
        + [_weight_spec(a, layer) for a in weights],
        out_specs=(_row_spec(tm), _meta_spec()),
        out_shape=_out_shapes(x, meta),
        scratch_shapes=[
            pltpu.VMEM((2 * HALF, CONV_WIDTH), f32),
            pltpu.VMEM((POOL_HIST, POOL_WIDTH), f32),
            pltpu.VMEM((HG_HEADS, HG_HEAD_DIM, HG_HEAD_DIM), f32),
            pltpu.VMEM((nc, CHUNK, HG_WIDTH), f32),
            pltpu.VMEM((nc, CHUNK, HG_WIDTH), f32),
            pltpu.VMEM((nc, CHUNK, HG_WIDTH), f32),
            pltpu.VMEM((nc, CHUNK, HG_WIDTH), f32),
            pltpu.VMEM((nc, CHUNK, HG_WIDTH), f32),
            pltpu.VMEM((tm, HG_WIDTH), f32),
            pltpu.VMEM((tm, D_MODEL), jnp.bfloat16),
            pltpu.VMEM((tm, D_MODEL), f32),
            pltpu.VMEM((tm, D_IN), f32),
            pltpu.VMEM((tm, D_MODEL), jnp.bfloat16),
        ],
        compiler_params=_compiler_params(),
        name=f"mixer_l{layer}",
    )(x, x, meta, *weights)


def _ffn_call(layer, x, meta, *weights):
    batch, rows, _ = x.shape
    tm = FFN_ROW_TILE
    f32 = jnp.float32
    return pl.pallas_call(
        _ffn_kernel,
        grid=(batch, rows // tm),
        in_specs=[_row_spec(tm), _meta_spec()] + [_weight_spec(a, layer) for a in weights],
        out_specs=(_row_spec(tm), _meta_spec()),
        out_shape=_out_shapes(x, meta),
        scratch_shapes=[
            pltpu.VMEM((tm, D_MODEL), jnp.bfloat16),
            pltpu.VMEM((FF_AHEAD_SLOTS, tm, 2 * FF_CHUNK), f32),
            pltpu.VMEM((N_FF_CHUNKS, HALF, 2 * FF_CHUNK), f32),
            pltpu.VMEM((2, tm, max(FF_GROUPS) * FF_CHUNK), jnp.bfloat16),
            pltpu.VMEM((tm, D_MODEL), f32),
        ],
        compiler_params=_compiler_params(),
        name=f"ffn_l{layer}",
    )(x, meta, *weights)


def kernel(x, meta_tokens, hg_lower_bounds, w_in, w_conv, w_pool, pool_scale, hg_norm_g, w_o, ln1_g, ln1_b,
           w_up, w_ffn_conv, b_ffn_conv, w_down, ln2_g, ln2_b):
    bf16 = jnp.bfloat16
    batch, seq, _ = x.shape
    assert seq % ROW_TILE == 0 and ROW_TILE % (CHUNK * CHUNK_UNROLL) == 0 and D_IN % PROJ_PIECE == 0
    assert seq % FFN_ROW_TILE == 0 and sum(FF_GROUPS) == N_FF_CHUNKS
    assert POOL_HIST <= N_META <= CHUNK and N_META % (2 * HALF) == 0 and POOL_HIST % HALF == 0
    h = x
    meta = jnp.broadcast_to(meta_tokens[None].astype(x.dtype), (batch, N_META, D_MODEL))
    hb = hg_lower_bounds.astype(jnp.float32)
    w_in_b, w_o_b, w_up_b, w_down_b = (a.astype(bf16) for a in (w_in, w_o, w_up, w_down))
    for l in range(DEPTH):
        wpool_bd = jax.scipy.linalg.block_diag(*[w_pool[l, g] for g in range(len(POOL_WINDOWS))])
        h, meta = _mixer_call(l, h, meta, w_in_b, wpool_bd.astype(bf16), w_o_b, hb, w_conv[l].T,
                              pool_scale[l][None], hg_norm_g[l][None], ln1_g[l][None], ln1_b[l][None])
        h, meta = _ffn_call(l, h, meta, w_up_b, w_down_b, w_ffn_conv[l].T, b_ffn_conv[l][None],
                            ln2_g[l][None], ln2_b[l][None])
    return h
```

```python
import functools

import jax
import jax.numpy as jnp
from jax import lax
from jax.experimental import pallas as pl
from jax.experimental.pallas import tpu as pltpu

D_MODEL = 1024
N_META = 16
CONV_WIDTH = 256
HG_WIDTH = 512
HG_HEAD_DIM = 128
HG_HEADS = 4
POOL_WIDTH = 256
POOL_WINDOWS = (2, 4, 8, 16)
POOL_GROUP_DIM = 64
D_FF = 2816
D_IN = 3 * CONV_WIDTH + 4 * HG_WIDTH + POOL_WIDTH
DEPTH = 2
ALPHA = (2 * DEPTH) ** 0.25
LN_EPS = 1e-5
RMS_EPS = 1e-6
F_FLOOR = 1e-30
LOG2_E = 1.4426950408889634

ROW_TILE = 512
FFN_ROW_TILE = 512
CHUNK = 64
HALF = 8
POOL_HIST = max(POOL_WINDOWS)
CHUNK_UNROLL = 2
PROJ_PIECE = 256
FF_CHUNK = 256
N_FF_CHUNKS = D_FF // FF_CHUNK
FF_GROUPS = (6, 3, 2)
FF_AHEAD = 2
FF_AHEAD_SLOTS = FF_AHEAD + 1
VMEM_LIMIT_BYTES = 56 * 1024 * 1024

_C_CONV = 0
_C_HG = 3 * CONV_WIDTH
_C_POOL = _C_HG + 4 * HG_WIDTH


def _layer_norm(r, g, b):
    mu = jnp.mean(r, axis=-1, keepdims=True)
    c = r - mu
    var = jnp.mean(c * c, axis=-1, keepdims=True)
    return c * lax.rsqrt(var + LN_EPS) * g + b


def _sigmoid(x):
    return 1.0 / (1.0 + jnp.exp(-x))


def _shifted_rows(u, prev8, k):
    hist = jnp.concatenate([prev8, u[0:HALF]], axis=0)
    head = pltpu.roll(hist, k, 0)[HALF:2 * HALF]
    if u.shape[0] == HALF:
        return head
    return jnp.concatenate([head, pltpu.roll(u, k, 0)[HALF:]], axis=0)


def _mixer_rows(layer, rows, is_meta, x, proj, between, w, s):
    f32, bf16 = jnp.float32, jnp.bfloat16
    cl = min(CHUNK, rows)
    nc = rows // cl
    tiles = cl // HALF
    between = list(between)

    hh = proj(_C_HG, _C_POOL)
    hq = hh[:, 0:HG_WIDTH]
    fz = hh[:, HG_WIDTH:2 * HG_WIDTH]
    hi = hh[:, 2 * HG_WIDTH:3 * HG_WIDTH]
    gz = hh[:, 3 * HG_WIDTH:4 * HG_WIDTH]

    hb = w.hb[...]
    e = jnp.exp(hb - jnp.max(hb, axis=0, keepdims=True))
    p = e / jnp.sum(e, axis=0, keepdims=True)
    lb = jnp.sum(p[0:layer + 1], axis=0, keepdims=True) - p[0:1]

    sig = _sigmoid(fz)
    f = lb + (1.0 - lb) * sig
    g = (jnp.log(jnp.maximum(f, F_FLOOR)) * LOG2_E).reshape(rows // HALF, HALF, HG_WIDTH)
    rid3 = lax.broadcasted_iota(jnp.int32, g.shape, 1)
    sh = 1
    while sh < HALF:
        g = g + jnp.where(rid3 >= sh, pltpu.roll(g, sh, 1), 0.0)
        sh *= 2
    g = g.reshape(nc, tiles, HALF, HG_WIDTH)
    run = g[:, 0]
    s.g[0:nc, 0:HALF, :] = run
    for j in range(1, tiles):
        run = g[:, j] + run[:, HALF - 1:HALF, :]
        s.g[0:nc, j * HALF:(j + 1) * HALF, :] = run
    s.q[0:nc, 0:cl] = (hq * (HG_HEAD_DIM ** -0.5)).reshape(nc, cl, HG_WIDTH)
    s.k[0:nc, 0:cl] = ((1.0 - lb) * (1.0 - sig)).reshape(nc, cl, HG_WIDTH)
    s.i[0:nc, 0:cl] = hi.reshape(nc, cl, HG_WIDTH)
    s.gate[0:rows] = gz * _sigmoid(gz)

    hc = proj(_C_CONV, _C_HG)
    cb = hc[:, 0:CONV_WIDTH]
    z = hc[:, CONV_WIDTH:2 * CONV_WIDTH] * hc[:, 2 * CONV_WIDTH:3 * CONV_WIDTH]
    zprev = s.zcarry[0:HALF]
    s.zcarry[0:HALF] = z[rows - HALF:rows]
    wc = w.wconv[...]
    conv = wc[0:1] * _shifted_rows(z, zprev, 2) + wc[1:2] * _shifted_rows(z, zprev, 1) + wc[2:3] * z
    s.ymix[0:rows, 0:CONV_WIDTH] = (cb * conv).astype(bf16)

    pv = proj(_C_POOL, D_IN)
    ext = jnp.concatenate([s.vcarry[...], pv], axis=0)
    s.vcarry[...] = pv[rows - POOL_HIST:rows]
    sums = {1: ext}
    width = 1
    while width < POOL_HIST:
        sums[2 * width] = sums[width] + pltpu.roll(sums[width], width, 0)
        width *= 2
    lane = lax.broadcasted_iota(jnp.int32, (1, POOL_WIDTH), 1)
    win = jnp.full((1, POOL_WIDTH), POOL_WINDOWS[-1], f32)
    wsum = sums[POOL_WINDOWS[-1]][POOL_HIST:]
    for gi in reversed(range(len(POOL_WINDOWS) - 1)):
        in_lower_groups = lane < (gi + 1) * POOL_GROUP_DIM
        win = jnp.where(in_lower_groups, float(POOL_WINDOWS[gi]), win)
        wsum = jnp.where(in_lower_groups, sums[POOL_WINDOWS[gi]][POOL_HIST:], wsum)
    if is_meta:
        pos = lax.broadcasted_iota(jnp.int32, (rows, 1), 0).astype(f32)
        count = jnp.minimum(pos + 1.0, win)
    else:
        count = win
    dpool = wsum / count - pv
    ypool = jnp.dot(dpool.astype(bf16), w.wpool[...], preferred_element_type=f32) * w.pscale[...]
    s.ymix[0:rows, CONV_WIDTH + HG_WIDTH:D_MODEL] = ypool.astype(bf16)

    s.mix[0:rows] = (
        jnp.dot(s.ymix[0:rows, 0:CONV_WIDTH], w.wo[0:CONV_WIDTH, :], preferred_element_type=f32)
        + jnp.dot(s.ymix[0:rows, CONV_WIDTH + HG_WIDTH:D_MODEL], w.wo[CONV_WIDTH + HG_WIDTH:D_MODEL, :],
                  preferred_element_type=f32))

    rid = lax.broadcasted_iota(jnp.int32, (HALF, 1), 0)
    sgn4 = jnp.where(rid < 4, -1.0, 1.0).astype(f32)
    sgn2 = jnp.where((rid & 2) == 0, -1.0, 1.0).astype(f32)
    odd = (rid & 1) == 1
    lo4 = rid < 4
    nst = HG_HEADS * cl
    tt = lax.broadcasted_iota(jnp.int32, (nst, nst), 0)
    uu = lax.broadcasted_iota(jnp.int32, (nst, nst), 1)
    levels = [1 << b for b in range(cl.bit_length() - 1)]
    masks = [((tt // (2 * m)) == (uu // (2 * m))) & ((tt % (2 * m)) >= m) & ((uu % (2 * m)) < m)
             for m in levels]
    diag = tt == uu
    contract_lanes = (((1,), (1,)), ((), ()))
    head_lanes = [slice(h * HG_HEAD_DIM, (h + 1) * HG_HEAD_DIM) for h in range(HG_HEADS)]

    def stack_heads(v):
        return jnp.concatenate([v[:, ls] for ls in head_lanes], axis=0)

    def level_exponents(c, g_tiles, m):
        out = []
        for j, gt in enumerate(g_tiles):
            r0 = j * HALF
            if m >= HALF:
                ref = (r0 // (2 * m)) * 2 * m + m - 1
                gref = s.g[c, ref:ref + 1, :]
                out.append(gref - gt if (r0 % (2 * m)) < m else gt - gref)
            elif m == 4:
                out.append(sgn4 * (gt - s.g[c, r0 + 3:r0 + 4, :]))
            elif m == 2:
                out.append(sgn2 * (gt - jnp.where(lo4, s.g[c, r0 + 1:r0 + 2, :], s.g[c, r0 + 5:r0 + 6, :])))
            else:
                out.append(jnp.where(odd, gt - pltpu.roll(gt, 1, 0), 0.0))
        return jnp.concatenate(out, axis=0)

    def emit_between():
        if between:
            between.pop(0)()

    def process_chunks(chunks):
        gcs = [s.g[c, 0:cl, :] for c in chunks]
        ibs = [s.i[c, 0:cl, :].astype(bf16) for c in chunks]
        states = [[s.st[h]] for h in range(HG_HEADS)]
        for n, c in enumerate(chunks):
            gl = gcs[n][cl - 1:cl]
            kt = (s.k[c, 0:cl, :] * jnp.exp2(gl - gcs[n])).astype(bf16)
            decay = jnp.exp2(gl)
            for h, ls in enumerate(head_lanes):
                delta = lax.dot_general(ibs[n][:, ls], kt[:, ls], (((0,), (0,)), ((), ())),
                                        preferred_element_type=f32)
                states[h].append(decay[:, ls] * states[h][-1] + delta)
        for h in range(HG_HEADS):
            s.st[h] = states[h][-1]
        emit_between()
        amats = []
        for n, c in enumerate(chunks):
            g_tiles = [gcs[n][j * HALF:(j + 1) * HALF] for j in range(tiles)]
            qb = s.q[c, 0:cl, :].astype(bf16)
            kf = s.k[c, 0:cl, :]
            qf = s.q[c, 0:cl, :]
            a = jnp.where(diag, jnp.sum(stack_heads(qf * kf), axis=-1, keepdims=True), 0.0)
            e1 = jnp.exp2(level_exponents(c, g_tiles, 1))
            a = jnp.where(masks[0], jnp.sum(stack_heads(qf * e1 * pltpu.roll(kf, 1, 0)), axis=-1, keepdims=True), a)
            for m, mask in zip(levels[1:], masks[1:]):
                ef = jnp.exp2(level_exponents(c, g_tiles, m))
                a = jnp.where(mask, jnp.dot(stack_heads(qb * ef.astype(bf16)), stack_heads(kf * ef).T.astype(bf16),
                                            preferred_element_type=f32), a)
            amats.append(a.astype(bf16))
        emit_between()
        inters = []
        for n, c in enumerate(chunks):
            qt = (s.q[c, 0:cl, :] * jnp.exp2(gcs[n])).astype(bf16)
            inters.append([jnp.dot(qt[:, ls], states[h][n].T.astype(bf16), preferred_element_type=f32)
                           for h, ls in enumerate(head_lanes)])
        emit_between()
        for n, c in enumerate(chunks):
            intra = jnp.dot(amats[n], stack_heads(ibs[n]), preferred_element_type=f32)
            for h, ls in enumerate(head_lanes):
                s.o[c, 0:cl, ls] = inters[n][h] + intra[h * cl:(h + 1) * cl]

    for c0 in range(0, nc, CHUNK_UNROLL):
        process_chunks(list(range(c0, min(c0 + CHUNK_UNROLL, nc))))
    while between:
        emit_between()

    gn = w.gnorm[...]
    for h in range(HG_HEADS):
        ls = slice(h * HG_HEAD_DIM, (h + 1) * HG_HEAD_DIM)
        o = s.o[0:nc, 0:cl, ls].reshape(rows, HG_HEAD_DIM)
        o = o * lax.rsqrt(jnp.mean(o * o, axis=-1, keepdims=True) + RMS_EPS) * gn
        o = o * s.gate[0:rows, ls]
        s.ymix[0:rows, CONV_WIDTH + h * HG_HEAD_DIM:CONV_WIDTH + (h + 1) * HG_HEAD_DIM] = o.astype(bf16)

    mix = s.mix[0:rows] + jnp.dot(s.ymix[0:rows, CONV_WIDTH:CONV_WIDTH + HG_WIDTH],
                                  w.wo[CONV_WIDTH:CONV_WIDTH + HG_WIDTH, :], preferred_element_type=f32)
    return _layer_norm(ALPHA * x + mix, w.lng[...], w.lnb[...])


class _Refs:
    def __init__(self, **kw):
        self.__dict__.update(kw)


def _mixer_kernel(layer, x_ref, xnext_ref, meta_ref, win, wpool, wo, hb, wconv, pscale, gnorm, lng, lnb,
                  out_ref, mout_ref,
                  zcarry, vcarry, st, q, k, i, g, o, gate, ymix, mix, hproj, xnb):
    f32, bf16 = jnp.float32, jnp.bfloat16
    w = _Refs(hb=hb, win=win, wconv=wconv, wpool=wpool, pscale=pscale, gnorm=gnorm, wo=wo, lng=lng, lnb=lnb)
    s = _Refs(zcarry=zcarry, vcarry=vcarry, st=st, q=q, k=k, i=i, g=g, o=o, gate=gate, ymix=ymix, mix=mix)
    first_of_batch = pl.program_id(1) == 0

    @pl.when(first_of_batch & (pl.program_id(0) == 0))
    def _():
        hproj[...] = jnp.dot(x_ref[0].astype(bf16), win[...], preferred_element_type=f32)

    @pl.when(first_of_batch)
    def _():
        zcarry[...] = jnp.zeros(zcarry.shape, f32)
        vcarry[...] = jnp.zeros(vcarry.shape, f32)
        st[...] = jnp.zeros(st.shape, f32)
        mb = meta_ref[0].astype(bf16)
        mout_ref[0] = _mixer_rows(
            layer, N_META, True, meta_ref[0],
            lambda c0, c1: jnp.dot(mb, win[:, c0:c1], preferred_element_type=f32), [], w, s)

    xnb[...] = xnext_ref[0].astype(bf16)

    def piece(c0):
        def run():
            hproj[:, c0:c0 + PROJ_PIECE] = jnp.dot(xnb[...], win[:, c0:c0 + PROJ_PIECE], preferred_element_type=f32)
        return run

    out_ref[0] = _mixer_rows(layer, ROW_TILE, False, x_ref[0], lambda c0, c1: hproj[:, c0:c1],
                             [piece(c0) for c0 in range(0, D_IN, PROJ_PIECE)], w, s)


def _ffn_rows(rows, x, w, s):
    f32, bf16 = jnp.float32, jnp.bfloat16
    s.xb[0:rows] = x.astype(bf16)

    def up_project(j):
        for part in range(2):
            c0 = part * D_FF + j * FF_CHUNK
            s.u[j % FF_AHEAD_SLOTS, 0:rows, part * FF_CHUNK:(part + 1) * FF_CHUNK] = jnp.dot(
                s.xb[0:rows], w.wup[:, c0:c0 + FF_CHUNK], preferred_element_type=f32)

    def conv_rows(j, part):
        c0 = part * D_FF + j * FF_CHUNK
        u = s.u[j % FF_AHEAD_SLOTS, 0:rows, part * FF_CHUNK:(part + 1) * FF_CHUNK]
        prev8 = s.ucarry[j, :, part * FF_CHUNK:(part + 1) * FF_CHUNK]
        s.ucarry[j, :, part * FF_CHUNK:(part + 1) * FF_CHUNK] = u[rows - HALF:rows]
        wc = w.wcv[:, c0:c0 + FF_CHUNK]
        return (wc[0:1] * _shifted_rows(u, prev8, 2) + wc[1:2] * _shifted_rows(u, prev8, 1) + wc[2:3] * u
                + w.bcv[:, c0:c0 + FF_CHUNK])

    starts = [sum(FF_GROUPS[:g]) for g in range(len(FF_GROUPS) + 1)]
    group_of = {j: g for g in range(len(FF_GROUPS)) for j in range(starts[g], starts[g + 1])}

    def down_project(grp):
        j0, j1 = starts[grp], starts[grp + 1]
        return jnp.dot(s.a[grp % 2, 0:rows, 0:(j1 - j0) * FF_CHUNK], w.wdn[j0 * FF_CHUNK:j1 * FF_CHUNK, :],
                       preferred_element_type=f32)

    for j in range(FF_AHEAD):
        up_project(j)
    pending = None
    for j in range(N_FF_CHUNKS):
        if j + FF_AHEAD < N_FF_CHUNKS:
            up_project(j + FF_AHEAD)
        if pending is not None:
            if pending == 0:
                s.acc[0:rows] = down_project(pending)
            else:
                s.acc[0:rows] += down_project(pending)
            pending = None
        gate = conv_rows(j, 0)
        val = conv_rows(j, 1)
        grp = group_of[j]
        k = j - starts[grp]
        s.a[grp % 2, 0:rows, k * FF_CHUNK:(k + 1) * FF_CHUNK] = (gate * _sigmoid(gate) * val).astype(bf16)
        if j + 1 == starts[grp + 1] and grp < len(FF_GROUPS) - 1:
            pending = grp
    return _layer_norm(ALPHA * x + s.acc[0:rows] + down_project(len(FF_GROUPS) - 1), w.lng[...], w.lnb[...])


def _ffn_kernel(x_ref, meta_ref, wup, wdn, wcv, bcv, lng, lnb, out_ref, mout_ref,
                xb, u, ucarry, a, acc):
    w = _Refs(wup=wup, wcv=wcv, bcv=bcv, wdn=wdn, lng=lng, lnb=lnb)
    s = _Refs(xb=xb, u=u, ucarry=ucarry, a=a, acc=acc)

    @pl.when(pl.program_id(1) == 0)
    def _():
        ucarry[...] = jnp.zeros(ucarry.shape, jnp.float32)
        mout_ref[0] = _ffn_rows(N_META, meta_ref[0], w, s)

    out_ref[0] = _ffn_rows(FFN_ROW_TILE, x_ref[0], w, s)


def _weight_spec(a, layer):
    if a.ndim == 3:
        return pl.BlockSpec((None,) + a.shape[1:], lambda b, i: (layer, 0, 0), pipeline_mode=pl.Buffered(1))
    zeros = (0,) * a.ndim
    return pl.BlockSpec(a.shape, lambda b, i: zeros, pipeline_mode=pl.Buffered(1))


def _row_spec(tm):
    return pl.BlockSpec((1, tm, D_MODEL), lambda b, i: (b, i, 0))


def _next_row_spec(batch, tiles_per_seq):
    last = batch * tiles_per_seq - 1

    def index(b, i):
        nxt = jnp.minimum(b * tiles_per_seq + i + 1, last)
        return (nxt // tiles_per_seq, nxt % tiles_per_seq, 0)

    return pl.BlockSpec((1, ROW_TILE, D_MODEL), index)


def _meta_spec():
    return pl.BlockSpec((1, N_META, D_MODEL), lambda b, i: (b, 0, 0))


def _compiler_params():
    return pltpu.CompilerParams(dimension_semantics=("arbitrary", "arbitrary"),
                                vmem_limit_bytes=VMEM_LIMIT_BYTES)


def _out_shapes(x, meta):
    return (jax.ShapeDtypeStruct(x.shape, jnp.float32), jax.ShapeDtypeStruct(meta.shape, jnp.float32))


def _mixer_call(layer, x, meta, *weights):
    batch, rows, _ = x.shape
    tm = ROW_TILE
    nc = tm // CHUNK
    f32 = jnp.float32
    return pl.pallas_call(
        functools.partial(_mixer_kernel, layer),
        grid=(batch, rows // tm),
        in_specs=[_row_spec(tm), _next_row_spec(batch, rows // tm), _meta_spec()]
        + [_weight_spec(a, layer) for a in weights],
        out_specs=(_row_spec(tm), _meta_spec()),
        out_shape=_out_shapes(x, meta),
        scratch_shapes=[
            pltpu.VMEM((2 * HALF, CONV_WIDTH), f32),
            pltpu.VMEM((POOL_HIST, POOL_WIDTH), f32),
            pltpu.VMEM((HG_HEADS, HG_HEAD_DIM, HG_HEAD_DIM), f32),
            pltpu.VMEM((nc, CHUNK, HG_WIDTH), f32),
            pltpu.VMEM((nc, CHUNK, HG_WIDTH), f32),
            pltpu.VMEM((nc, CHUNK, HG_WIDTH), f32),
            pltpu.VMEM((nc, CHUNK, HG_WIDTH), f32),
            pltpu.VMEM((nc, CHUNK, HG_WIDTH), f32),
            pltpu.VMEM((tm, HG_WIDTH), f32),
            pltpu.VMEM((tm, D_MODEL), jnp.bfloat16),
            pltpu.VMEM((tm, D_MODEL), f32),
            pltpu.VMEM((tm, D_IN), f32),
            pltpu.VMEM((tm, D_MODEL), jnp.bfloat16),
        ],
        compiler_params=_compiler_params(),
        name=f"mixer_l{layer}",
    )(x, x, meta, *weights)


def _ffn_call(layer, x, meta, *weights):
    batch, rows, _ = x.shape
    tm = FFN_ROW_TILE
    f32 = jnp.float32
    return pl.pallas_call(
        _ffn_kernel,
        grid=(batch, rows // tm),
        in_specs=[_row_spec(tm), _meta_spec()] + [_weight_spec(a, layer) for a in weights],
        out_specs=(_row_spec(tm), _meta_spec()),
        out_shape=_out_shapes(x, meta),
        scratch_shapes=[
            pltpu.VMEM((tm, D_MODEL), jnp.bfloat16),
            pltpu.VMEM((FF_AHEAD_SLOTS, tm, 2 * FF_CHUNK), f32),
            pltpu.VMEM((N_FF_CHUNKS, HALF, 2 * FF_CHUNK), f32),
            pltpu.VMEM((2, tm, max(FF_GROUPS) * FF_CHUNK), jnp.bfloat16),
            pltpu.VMEM((tm, D_MODEL), f32),
        ],
        compiler_params=_compiler_params(),
        name=f"ffn_l{layer}",
    )(x, meta, *weights)


def kernel(x, meta_tokens, hg_lower_bounds, w_in, w_conv, w_pool, pool_scale, hg_norm_g, w_o, ln1_g, ln1_b,
           w_up, w_ffn_conv, b_ffn_conv, w_down, ln2_g, ln2_b):
    bf16 = jnp.bfloat16
    batch, seq, _ = x.shape
    assert seq % ROW_TILE == 0 and ROW_TILE % (CHUNK * CHUNK_UNROLL) == 0 and D_IN % PROJ_PIECE == 0
    assert seq % FFN_ROW_TILE == 0 and sum(FF_GROUPS) == N_FF_CHUNKS
    assert POOL_HIST <= N_META <= CHUNK and N_META % (2 * HALF) == 0 and POOL_HIST % HALF == 0
    h = x
    meta = jnp.broadcast_to(meta_tokens[None].astype(x.dtype), (batch, N_META, D_MODEL))
    hb = hg_lower_bounds.astype(jnp.float32)
    w_in_b, w_o_b, w_up_b, w_down_b = (a.astype(bf16) for a in (w_in, w_o, w_up, w_down))
    for l in range(DEPTH):
        wpool_bd = jax.scipy.linalg.block_diag(*[w_pool[l, g] for g in range(len(POOL_WINDOWS))])
        h, meta = _mixer_call(l, h, meta, w_in_b, wpool_bd.astype(bf16), w_o_b, hb, w_conv[l].T,
                              pool_scale[l][None], hg_norm_g[l][None], ln1_g[l][None], ln1_b[l][None])
        h, meta = _ffn_call(l, h, meta, w_up_b, w_down_b, w_ffn_conv[l].T, b_ffn_conv[l][None],
                            ln2_g[l][None], ln2_b[l][None])
    return h
```
